```python
import jax
import jax.numpy as jnp
from jax import lax
import numpy as np

D_MODEL = 1024
BATCH = 8
SEQ = 4096
DEPTH = 1

GRID_W = 64
EPS = 1e-6
ATTN_HEADS = 8
ATTN_KV_HEADS = 2
ATTN_HEAD_DIM = 64
ATTN_WIDTH = ATTN_HEADS * ATTN_HEAD_DIM
KV_WIDTH = ATTN_KV_HEADS * ATTN_HEAD_DIM
Q_BLOCK = 128
ROPE_THETA = 10000.0
MLSTM_HEADS = 4
MLSTM_HEAD_DIM = 128
MLSTM_WIDTH = MLSTM_HEADS * MLSTM_HEAD_DIM
MLSTM_CHUNK = 64
CONV_WIDTH = 5
N_GATES = 4 * MLSTM_HEADS
FORGET_BIAS_LO = 3.0
FORGET_BIAS_HI = 6.0
MIX_WIDTH = ATTN_WIDTH + MLSTM_WIDTH
IN_PROJ_WIDTH = ATTN_WIDTH + 2 * KV_WIDTH + 4 * MLSTM_WIDTH + N_GATES
FFN_HIDDEN = 256 * ((8 * D_MODEL + 3 * 256 - 1) // (3 * 256))

kernel_name = 'hymba_axial_gqa_bi_mlstm_adaln_block'


def rms_norm(x, g):
    xf = x.astype(jnp.float32)
    y = xf * lax.rsqrt(jnp.mean(xf * xf, axis=-1, keepdims=True) + EPS)
    return (y * g.astype(jnp.float32)).astype(x.dtype)


def modulate(h, shift, scale):
    return h * (1.0 + scale[:, None, :]) + shift[:, None, :]


def axial_rope_tables(seq):
    rows = seq // GRID_W
    row = jnp.repeat(jnp.arange(rows, dtype=jnp.float32), GRID_W)
    col = jnp.tile(jnp.arange(GRID_W, dtype=jnp.float32), rows)
    axis_dim = ATTN_HEAD_DIM // 2
    inv_freq = ROPE_THETA ** (-jnp.arange(0, axis_dim, 2, dtype=jnp.float32) / axis_dim)
    ang_row = row[:, None] * inv_freq[None, :]
    ang_col = col[:, None] * inv_freq[None, :]
    return (jnp.cos(ang_row), jnp.sin(ang_row), jnp.cos(ang_col), jnp.sin(ang_col))


def rotate_pairs(x, cos, sin):
    x1, x2 = jnp.split(x, 2, axis=-1)
    cos = cos[:, None, :]
    sin = sin[:, None, :]
    return jnp.concatenate([x1 * cos - x2 * sin, x2 * cos + x1 * sin], axis=-1)


def apply_axial_rope(x, tables):
    cos_r, sin_r, cos_c, sin_c = tables
    x_row, x_col = jnp.split(x.astype(jnp.float32), 2, axis=-1)
    out = jnp.concatenate([rotate_pairs(x_row, cos_r, sin_r), rotate_pairs(x_col, cos_c, sin_c)], axis=-1)
    return out.astype(x.dtype)


def axial_gqa_attention(q, k, v, q_g, k_g, tables):
    B, S, _ = q.shape
    G = ATTN_HEADS // ATTN_KV_HEADS
    q = q.reshape(B, S, ATTN_HEADS, ATTN_HEAD_DIM)
    k = k.reshape(B, S, ATTN_KV_HEADS, ATTN_HEAD_DIM)
    v = v.reshape(B, S, ATTN_KV_HEADS, ATTN_HEAD_DIM)
    q = apply_axial_rope(rms_norm(q, q_g), tables)
    k = apply_axial_rope(rms_norm(k, k_g), tables)
    n_blocks = S // Q_BLOCK
    qb = q.reshape(B, n_blocks, Q_BLOCK, ATTN_KV_HEADS, G, ATTN_HEAD_DIM).transpose(1, 0, 3, 4, 2, 5)
    kt = k.transpose(0, 2, 1, 3)
    vt = v.transpose(0, 2, 1, 3)
    scale = ATTN_HEAD_DIM ** -0.5

    def one_block(q_blk):
        s = jnp.einsum('bkgqd,bksd->bkgqs', q_blk, kt).astype(jnp.float32) * scale
        p = jax.nn.softmax(s, axis=-1).astype(vt.dtype)
        return jnp.einsum('bkgqs,bksd->bkgqd', p, vt)

    o = lax.map(one_block, qb)
    return o.transpose(1, 0, 4, 2, 3, 5).reshape(B, S, ATTN_WIDTH)


def mlstm_chunkwise(q, k, v, i_pre, log_f):
    B, H, S, D = q.shape
    L = MLSTM_CHUNK
    nc = S // L

    def to_chunks(a):
        return jnp.moveaxis(a.reshape((B, H, nc, L) + a.shape[3:]), 2, 0)

    xs = (to_chunks(q), to_chunks(k), to_chunks(v), to_chunks(i_pre), to_chunks(log_f))
    tril = jnp.tril(jnp.ones((L, L), dtype=bool))

    def step(carry, chunk):
        C, n, m = carry
        qc, kc, vc, ic, fc = chunk
        b = jnp.cumsum(fc, axis=-1)
        g = b[..., -1]
        d = b[..., :, None] - b[..., None, :] + ic[..., None, :]
        d = jnp.where(tril, d, -jnp.inf)
        m_inter = b + m[..., None]
        m_t = jnp.maximum(m_inter, jnp.max(d, axis=-1))
        w_inter = jnp.exp(m_inter - m_t)
        p = jnp.exp(d - m_t[..., None])
        s = jnp.einsum('bhtd,bhsd->bhts', qc, kc) * p
        num = w_inter[..., None] * jnp.einsum('bhvd,bhtd->bhtv', C, qc) + jnp.einsum('bhts,bhsv->bhtv', s, vc)
        den = w_inter * jnp.einsum('bhd,bhtd->bht', n, qc) + jnp.sum(s, axis=-1)
        h = num / jnp.maximum(jnp.abs(den), jnp.exp(-m_t))[..., None]
        a = g[..., None] - b + ic
        m_new = jnp.maximum(g + m, jnp.max(a, axis=-1))
        decay = jnp.exp(g + m - m_new)
        w = jnp.exp(a - m_new[..., None])
        C = decay[..., None, None] * C + jnp.einsum('bhs,bhsv,bhsd->bhvd', w, vc, kc)
        n = decay[..., None] * n + jnp.einsum('bhs,bhsd->bhd', w, kc)
        return (C, n, m_new), h

    init = (jnp.zeros((B, H, D, D), jnp.float32), jnp.zeros((B, H, D), jnp.float32), jnp.zeros((B, H), jnp.float32))
    _, h = lax.scan(step, init, xs)
    return jnp.moveaxis(h, 0, 2).reshape(B, H, S, D)


def centred_depthwise_conv(x, w, b):
    ch = x.shape[-1]
    y = lax.conv_general_dilated(x, w[:, None, :].astype(x.dtype), window_strides=(1,),
                                 padding=[(CONV_WIDTH // 2, CONV_WIDTH // 2)],
                                 dimension_numbers=('NWC', 'WIO', 'NWC'), feature_group_count=ch)
    return y + b.astype(x.dtype)


def bidirectional_mlstm(qk_pre, v, o_pre, gate_pre, conv_w, conv_b, gate_b, norm_g):
    B, S, _ = v.shape
    qk = jax.nn.silu(centred_depthwise_conv(qk_pre, conv_w, conv_b))
    q, k = jnp.split(qk, 2, axis=-1)

    def heads(a):
        return a.astype(jnp.float32).reshape(B, S, MLSTM_HEADS, MLSTM_HEAD_DIM).transpose(0, 2, 1, 3)

    qh = heads(q)
    kh = heads(k) * MLSTM_HEAD_DIM ** -0.5
    vh = heads(v)
    g = (gate_pre.astype(jnp.float32) + gate_b.astype(jnp.float32)).reshape(B, S, 4, MLSTM_HEADS)
    g = g.transpose(2, 0, 3, 1)
    i_fw, f_fw, i_bw, f_bw = g[0], g[1], g[2], g[3]
    h_fw = mlstm_chunkwise(qh, kh, vh, i_fw, jax.nn.log_sigmoid(f_fw))
    flip = lambda a: jnp.flip(a, axis=2)
    h_bw = flip(mlstm_chunkwise(flip(qh), flip(kh), flip(vh), flip(i_bw), flip(jax.nn.log_sigmoid(f_bw))))
    h = (h_fw + h_bw).transpose(0, 2, 1, 3)
    h = h * lax.rsqrt(jnp.mean(h * h, axis=-1, keepdims=True) + EPS)
    h = h * norm_g.astype(jnp.float32).reshape(MLSTM_HEADS, MLSTM_HEAD_DIM)
    h = h.reshape(B, S, MLSTM_WIDTH) * jax.nn.sigmoid(o_pre.astype(jnp.float32))
    return h.astype(v.dtype)


def setup_inputs(seed: int = 0) -> dict:
    key = jax.random.key(seed)
    ks = jax.random.split(key, 18)
    nrm = lambda k, shape, s: jax.random.normal(k, shape, jnp.float32) * s
    gain = lambda k, shape: 1.0 + nrm(k, shape, 0.02)
    f_bias = jnp.linspace(FORGET_BIAS_LO, FORGET_BIAS_HI, MLSTM_HEADS, dtype=jnp.float32)
    gate_offset = jnp.zeros((4, MLSTM_HEADS), jnp.float32).at[1].set(f_bias).at[3].set(f_bias)
    gate_b = (nrm(ks[10], (DEPTH, 4, MLSTM_HEADS), 0.1) + gate_offset).reshape(DEPTH, N_GATES)
    return {
        'x': nrm(ks[0], (BATCH, SEQ, D_MODEL), 1.0),
        'c': nrm(ks[1], (BATCH, D_MODEL), 1.0),
        'w_ada': nrm(ks[2], (DEPTH, D_MODEL, 6 * D_MODEL), D_MODEL ** -0.5),
        'b_ada': nrm(ks[3], (DEPTH, 6 * D_MODEL), 0.02),
        'norm_mix_g': gain(ks[4], (DEPTH, D_MODEL)),
        'w_in': nrm(ks[5], (DEPTH, D_MODEL, IN_PROJ_WIDTH), D_MODEL ** -0.5),
        'q_norm_g': gain(ks[6], (DEPTH, ATTN_HEAD_DIM)),
        'k_norm_g': gain(ks[7], (DEPTH, ATTN_HEAD_DIM)),
        'conv_w': nrm(ks[8], (DEPTH, CONV_WIDTH, 2 * MLSTM_WIDTH), CONV_WIDTH ** -0.5),
        'conv_b': nrm(ks[9], (DEPTH, 2 * MLSTM_WIDTH), 0.02),
        'gate_b': gate_b,
        'mlstm_norm_g': gain(ks[11], (DEPTH, MLSTM_WIDTH)),
        'w_out': nrm(ks[12], (DEPTH, MIX_WIDTH, D_MODEL), MIX_WIDTH ** -0.5),
        'norm_ffn_g': gain(ks[13], (DEPTH, D_MODEL)),
        'w_gate': nrm(ks[14], (DEPTH, D_MODEL, FFN_HIDDEN), D_MODEL ** -0.5),
        'w_up': nrm(ks[15], (DEPTH, D_MODEL, FFN_HIDDEN), D_MODEL ** -0.5),
        'w_down': nrm(ks[16], (DEPTH, FFN_HIDDEN, D_MODEL), FFN_HIDDEN ** -0.5),
        'final_norm_g': gain(ks[17], (D_MODEL,)),
    }


def reference(x, c, w_ada, b_ada, norm_mix_g, w_in, q_norm_g, k_norm_g, conv_w, conv_b, gate_b,
              mlstm_norm_g, w_out, norm_ffn_g, w_gate, w_up, w_down, final_norm_g):
    S = x.shape[1]
    tables = axial_rope_tables(S)
    cond = jax.nn.silu(c)
    splits = [ATTN_WIDTH,
              ATTN_WIDTH + KV_WIDTH,
              ATTN_WIDTH + 2 * KV_WIDTH,
              ATTN_WIDTH + 2 * KV_WIDTH + 2 * MLSTM_WIDTH,
              ATTN_WIDTH + 2 * KV_WIDTH + 3 * MLSTM_WIDTH,
              ATTN_WIDTH + 2 * KV_WIDTH + 4 * MLSTM_WIDTH]
    for l in range(DEPTH):
        mod = cond @ w_ada[l] + b_ada[l]
        sh1, sc1, g1, sh2, sc2, g2 = jnp.split(mod, 6, axis=-1)
        h = modulate(rms_norm(x, norm_mix_g[l]), sh1, sc1)
        proj = h @ w_in[l]
        q_a, k_a, v_a, qk_m, v_m, o_m, gates = jnp.split(proj, splits, axis=-1)
        y_attn = axial_gqa_attention(q_a, k_a, v_a, q_norm_g[l], k_norm_g[l], tables)
        y_mem = bidirectional_mlstm(qk_m, v_m, o_m, gates, conv_w[l], conv_b[l], gate_b[l], mlstm_norm_g[l])
        mix = jnp.concatenate([y_attn, y_mem], axis=-1) @ w_out[l]
        x = x + g1[:, None, :] * mix
        h = modulate(rms_norm(x, norm_ffn_g[l]), sh2, sc2)
        ffn = (jax.nn.silu(h @ w_gate[l]) * (h @ w_up[l])) @ w_down[l]
        x = x + g2[:, None, :] * ffn
    return rms_norm(x, final_norm_g)
```

```python
import functools

import jax
import jax.numpy as jnp
import numpy as np
from jax import lax
from jax.experimental import pallas as pl
from jax.experimental.pallas import tpu as pltpu

F32 = jnp.float32
BF16 = jnp.bfloat16

D_MODEL = 1024
GRID_W = 64
EPS = 1e-6
ATTN_HEADS = 8
ATTN_KV_HEADS = 2
ATTN_GROUP = ATTN_HEADS // ATTN_KV_HEADS
ATTN_HEAD_DIM = 64
ATTN_WIDTH = ATTN_HEADS * ATTN_HEAD_DIM
KV_WIDTH = ATTN_KV_HEADS * ATTN_HEAD_DIM
ROPE_THETA = 10000.0
MLSTM_HEADS = 4
MLSTM_HEAD_DIM = 128
MLSTM_WIDTH = MLSTM_HEADS * MLSTM_HEAD_DIM
CONV_WIDTH = 5
N_GATES = 4 * MLSTM_HEADS
MIX_WIDTH = ATTN_WIDTH + MLSTM_WIDTH
FFN_HIDDEN = 256 * ((8 * D_MODEL + 3 * 256 - 1) // (3 * 256))

LANES = 128
GATE_SLOTS = 8

TM_IN = 512
TM_FFN = 512
TQ = 128
TK = 256
CHUNK = 128
VMEM_LIMIT = 56 * 1024 * 1024


def _nt_dot(a, b):
    return lax.dot_general(a, b, (((1,), (1,)), ((), ())), preferred_element_type=F32)


def _tn_dot(a, b):
    return lax.dot_general(a, b, (((0,), (0,)), ((), ())), preferred_element_type=F32)


def _ada_kernel(c_ref, w_ref, b_ref, o_ref):
    c = c_ref[...]
    cond = c * jax.nn.sigmoid(c)
    o_ref[...] = jnp.dot(cond, w_ref[...], preferred_element_type=F32,
                         precision=lax.Precision.HIGHEST) + b_ref[...]


def _ada(c, w_ada, b_ada):
    B, D = c.shape
    n = w_ada.shape[1] // D
    return pl.pallas_call(
        _ada_kernel,
        grid=(n,),
        in_specs=[pl.BlockSpec((B, D), lambda j: (0, 0)),
                  pl.BlockSpec((D, D), lambda j: (0, j)),
                  pl.BlockSpec((1, D), lambda j: (0, j))],
        out_specs=pl.BlockSpec((B, D), lambda j: (0, j)),
        out_shape=jax.ShapeDtypeStruct((B, n * D), F32),
        name="ada",
    )(c, w_ada, b_ada.reshape(1, -1))


_C_Q = 0
_C_K = _C_Q + ATTN_WIDTH
_C_V = _C_K + KV_WIDTH
_C_QKM = _C_V + KV_WIDTH
_C_VM = _C_QKM + 2 * MLSTM_WIDTH
_C_OM = _C_VM + MLSTM_WIDTH
_C_G = _C_OM + MLSTM_WIDTH
_C_END = _C_G + LANES


def _inproj_kernel(x_ref, mod_ref, g_ref, w_ref, gb_ref, qg_ref, kg_ref, cos_ref, sa_ref, sb_ref, bd_ref,
                   q_ref, k_ref, vt_ref, qkm_ref, vm_ref, om_ref, gcol_ref, grow_ref):
    D = D_MODEL
    x = x_ref[0]
    ms = jnp.mean(x * x, axis=-1, keepdims=True)
    mod = mod_ref[0]
    sh1 = mod[:, 0:D]
    sc1 = mod[:, D:2 * D]
    h = (x * lax.rsqrt(ms + EPS)) * g_ref[...]
    h = h * (1.0 + sc1) + sh1
    hb = h.astype(BF16)

    cosf = cos_ref[...]
    sa = sa_ref[...]
    sb = sb_ref[...]
    bd = bd_ref[...]

    def norm_rope(xs, gain, scale):
        ss = jnp.dot((xs * xs).astype(BF16), bd, preferred_element_type=F32)
        xn = (xs * lax.rsqrt(ss + EPS)) * gain
        out = xn * cosf + pltpu.roll(xn, LANES - 16, 1) * sa + pltpu.roll(xn, 16, 1) * sb
        return out * scale

    hd = ATTN_HEAD_DIM
    for j in range(ATTN_WIDTH // LANES):
        qs = jnp.dot(hb, w_ref[:, _C_Q + j * LANES:_C_Q + (j + 1) * LANES], preferred_element_type=F32)
        qs = norm_rope(qs, qg_ref[...], hd ** -0.5).astype(BF16)
        q_ref[0, 2 * j] = qs[:, :hd]
        q_ref[0, 2 * j + 1] = qs[:, hd:]
    ks = jnp.dot(hb, w_ref[:, _C_K:_C_K + LANES], preferred_element_type=F32)
    ks = norm_rope(ks, kg_ref[...], 1.0).astype(BF16)
    k_ref[0, 0] = ks[:, :hd]
    k_ref[0, 1] = ks[:, hd:]
    vs = jnp.dot(hb, w_ref[:, _C_V:_C_V + LANES], preferred_element_type=F32)
    vt = vs.T.astype(BF16)
    vt_ref[0, 0] = vt[:hd]
    vt_ref[0, 1] = vt[hd:]
    qkm_ref[0] = jnp.dot(hb, w_ref[:, _C_QKM:_C_VM], preferred_element_type=F32)
    vm_ref[0] = jnp.dot(hb, w_ref[:, _C_VM:_C_OM], preferred_element_type=F32).astype(BF16)
    om_ref[0] = jnp.dot(hb, w_ref[:, _C_OM:_C_G], preferred_element_type=F32)
    gates = jnp.dot(hb, w_ref[:, _C_G:_C_END], preferred_element_type=F32) + gb_ref[...]
    gcol_ref[0] = gates
    gt = gates.T
    tm = gates.shape[0]
    grow_ref[0] = gt[:MLSTM_HEADS * GATE_SLOTS].reshape(MLSTM_HEADS, GATE_SLOTS, tm)


def _gate_perm():
    src = np.zeros((MLSTM_HEADS, 4), np.int32)
    for h in range(MLSTM_HEADS):
        for t in range(4):
            src[h, t] = t * MLSTM_HEADS + h
    return src


def _rope_tables(S):
    rows = S // GRID_W
    row = jnp.repeat(jnp.arange(rows, dtype=F32), GRID_W)
    col = jnp.tile(jnp.arange(GRID_W, dtype=F32), rows)
    axis_dim = ATTN_HEAD_DIM // 2
    inv_freq = ROPE_THETA ** (-jnp.arange(0, axis_dim, 2, dtype=F32) / axis_dim)
    ar = row[:, None] * inv_freq[None, :]
    ac = col[:, None] * inv_freq[None, :]
    cr, sr, cc, sc = jnp.cos(ar), jnp.sin(ar), jnp.cos(ac), jnp.sin(ac)
    z = jnp.zeros_like(sr)
    cos_h = jnp.concatenate([cr, cr, cc, cc], axis=-1)
    sa_h = jnp.concatenate([-sr, z, -sc, z], axis=-1)
    sb_h = jnp.concatenate([z, sr, z, sc], axis=-1)
    two = lambda a: jnp.concatenate([a, a], axis=-1)
    return two(cos_h), two(sa_h), two(sb_h)


def _inproj(x, mod3, norm_g, w_in, gate_b, q_g, k_g):
    B, S, D = x.shape
    tm = TM_IN
    H = MLSTM_HEADS
    perm = _gate_perm()
    wg = jnp.zeros((D, LANES), F32)
    gb = jnp.zeros((1, LANES), F32)
    gate0 = IN_PROJ_GATE0
    for h in range(H):
        wg = wg.at[:, GATE_SLOTS * h:GATE_SLOTS * h + 4].set(w_in[:, gate0 + perm[h]])
        gb = gb.at[0, GATE_SLOTS * h:GATE_SLOTS * h + 4].set(gate_b[perm[h]])
    w = jnp.concatenate([w_in[:, :gate0], wg], axis=1).astype(BF16)
    cosf, sa, sb = _rope_tables(S)
    qg2 = jnp.concatenate([q_g, q_g]).reshape(1, LANES)
    kg2 = jnp.concatenate([k_g, k_g]).reshape(1, LANES)
    blk = np.kron(np.eye(LANES // ATTN_HEAD_DIM), np.ones((ATTN_HEAD_DIM, ATTN_HEAD_DIM))) / ATTN_HEAD_DIM
    bd = jnp.asarray(blk, BF16)

    const = lambda shape: pl.BlockSpec(shape, lambda b, i: (0,) * len(shape))
    hd = ATTN_HEAD_DIM
    out_shape = (
        jax.ShapeDtypeStruct((B, ATTN_HEADS, S, hd), BF16),
        jax.ShapeDtypeStruct((B, ATTN_KV_HEADS, S, hd), BF16),
        jax.ShapeDtypeStruct((B, ATTN_KV_HEADS, hd, S), BF16),
        jax.ShapeDtypeStruct((B, S, 2 * MLSTM_WIDTH), F32),
        jax.ShapeDtypeStruct((B, S, MLSTM_WIDTH), BF16),
        jax.ShapeDtypeStruct((B, S, MLSTM_WIDTH), F32),
        jax.ShapeDtypeStruct((B, S, LANES), F32),
        jax.ShapeDtypeStruct((B, H, GATE_SLOTS, S), F32),
    )
    out_specs = (
        pl.BlockSpec((1, ATTN_HEADS, tm, hd), lambda b, i: (b, 0, i, 0)),
        pl.BlockSpec((1, ATTN_KV_HEADS, tm, hd), lambda b, i: (b, 0, i, 0)),
        pl.BlockSpec((1, ATTN_KV_HEADS, hd, tm), lambda b, i: (b, 0, 0, i)),
        pl.BlockSpec((1, tm, 2 * MLSTM_WIDTH), lambda b, i: (b, i, 0)),
        pl.BlockSpec((1, tm, MLSTM_WIDTH), lambda b, i: (b, i, 0)),
        pl.BlockSpec((1, tm, MLSTM_WIDTH), lambda b, i: (b, i, 0)),
        pl.BlockSpec((1, tm, LANES), lambda b, i: (b, i, 0)),
        pl.BlockSpec((1, H, GATE_SLOTS, tm), lambda b, i: (b, 0, 0, i)),
    )
    in_specs = [
        pl.BlockSpec((1, tm, D), lambda b, i: (b, i, 0)),
        pl.BlockSpec((1, 1, 6 * D), lambda b, i: (b, 0, 0)),
        const((1, D)),
        const((D, _C_END)),
        const((1, LANES)),
        const((1, LANES)),
        const((1, LANES)),
        pl.BlockSpec((tm, LANES), lambda b, i: (i, 0)),
        pl.BlockSpec((tm, LANES), lambda b, i: (i, 0)),
        pl.BlockSpec((tm, LANES), lambda b, i: (i, 0)),
        const((LANES, LANES)),
    ]
    return pl.pallas_call(
        _inproj_kernel,
        grid=(B, S // tm),
        in_specs=in_specs,
        out_specs=out_specs,
        out_shape=out_shape,
        compiler_params=pltpu.CompilerParams(
            dimension_semantics=("parallel", "parallel"), vmem_limit_bytes=VMEM_LIMIT),
        name="inproj",
    )(x, mod3, norm_g.reshape(1, D), w, gb, qg2, kg2, cosf, sa, sb, bd)


IN_PROJ_GATE0 = ATTN_WIDTH + 2 * KV_WIDTH + 4 * MLSTM_WIDTH


def _attn_kernel(q_ref, k_ref, vt_ref, o_ref):
    G = ATTN_GROUP
    hd = ATTN_HEAD_DIM
    S = k_ref.shape[2]
    R = G * TQ

    def qblock(qi, carry):
        r0 = pl.multiple_of(qi * TQ, TQ)
        qb = jnp.concatenate([q_ref[0, g, pl.ds(r0, TQ), :] for g in range(G)], axis=0)

        def kblock(kj, st):
            m, l, acc = st
            c0 = pl.multiple_of(kj * TK, TK)
            kb = k_ref[0, 0, pl.ds(c0, TK), :]
            s = _nt_dot(kb, qb)
            mn = jnp.maximum(m, jnp.max(s, axis=0, keepdims=True))
            alpha = jnp.exp(m - mn)
            p = jnp.exp(s - mn)
            l = alpha * l + jnp.sum(p, axis=0, keepdims=True)
            vb = vt_ref[0, 0, :, pl.ds(c0, TK)]
            acc = alpha * acc + jnp.dot(vb, p.astype(BF16), preferred_element_type=F32)
            return mn, l, acc

        init = (jnp.full((1, R), -jnp.inf, F32), jnp.zeros((1, R), F32), jnp.zeros((hd, R), F32))
        m, l, acc = lax.fori_loop(0, S // TK, kblock, init)
        o = (acc / l).astype(BF16)
        for g in range(G):
            o_ref[0, g * hd:(g + 1) * hd, pl.ds(r0, TQ)] = o[:, g * TQ:(g + 1) * TQ]
        return carry

    lax.fori_loop(0, S // TQ, qblock, 0)


def _attention(q, k, vt):
    B, _, S, hd = q.shape
    G = ATTN_GROUP
    return pl.pallas_call(
        _attn_kernel,
        grid=(B, ATTN_KV_HEADS),
        in_specs=[
            pl.BlockSpec((1, G, S, hd), lambda b, kv: (b, kv, 0, 0)),
            pl.BlockSpec((1, 1, S, hd), lambda b, kv: (b, kv, 0, 0)),
            pl.BlockSpec((1, 1, hd, S), lambda b, kv: (b, kv, 0, 0)),
        ],
        out_specs=pl.BlockSpec((1, G * hd, S), lambda b, kv: (b, kv, 0)),
        out_shape=jax.ShapeDtypeStruct((B, ATTN_WIDTH, S), BF16),
        compiler_params=pltpu.CompilerParams(
            dimension_semantics=("parallel", "parallel"), vmem_limit_bytes=VMEM_LIMIT),
        name="attn",
    )(q, k, vt)


def _log_sigmoid(x):
    return jnp.minimum(x, 0.0) - jnp.log(1.0 + jnp.exp(-jnp.abs(x)))


def _split_bf16(x):
    hi = x.astype(BF16)
    lo = (x - hi.astype(F32)).astype(BF16)
    return hi, lo


def _mlstm_kernel(qpre_ref, kpre_ref, v_ref, o_ref, gcol_ref, grow_ref, cwq_ref, cwk_ref, cbq_ref, cbk_ref,
                  ng_ref, y_ref, xp_ref, qs_ref, ks_ref, bc_ref, ac_ref, rows_ref, hf_ref, hb_ref, ct_ref):
    S = v_ref.shape[1]
    L = CHUNK
    nc = S // L
    Dh = MLSTM_HEAD_DIM
    head = pl.program_id(1)

    pad = 8
    zeros_pad = jnp.zeros((pad, LANES), F32)
    xp_ref[0:pad] = zeros_pad
    xp_ref[pad + S:pad + S + pad] = zeros_pad
    RB = 512
    for src_ref, cw_ref, cb_ref, dst_ref, scale in (
            (qpre_ref, cwq_ref, cbq_ref, qs_ref, 1.0),
            (kpre_ref, cwk_ref, cbk_ref, ks_ref, Dh ** -0.5)):
        xp_ref[pad:pad + S] = src_ref[0]
        cw = cw_ref[...]
        cb = cb_ref[...]
        for r in range(S // RB):
            acc = jnp.zeros((RB, LANES), F32) + cb
            for j in range(CONV_WIDTH):
                off = pad + r * RB + j - CONV_WIDTH // 2
                acc = acc + xp_ref[off:off + RB] * cw[j:j + 1]
            y = acc * jax.nn.sigmoid(acc)
            dst_ref[r * RB:(r + 1) * RB] = (y * scale).astype(BF16)

    lane = lax.broadcasted_iota(jnp.int32, (L, LANES), 1)
    ri = lax.broadcasted_iota(jnp.int32, (L, L), 0)
    ci = lax.broadcasted_iota(jnp.int32, (L, L), 1)
    tril_b = (ci <= ri).astype(BF16)
    triu_b = (ri <= ci).astype(BF16)
    shift = (LANES - GATE_SLOTS * head) % LANES

    def gate_chunk(c, carry):
        c0 = pl.multiple_of(c * L, L)
        gh = pltpu.roll(gcol_ref[0, pl.ds(c0, L), :], shift, 1)
        lf = _log_sigmoid(gh)
        hi, lo = _split_bf16(lf)
        cum = jnp.dot(tril_b, hi, preferred_element_type=F32) + jnp.dot(tril_b, lo, preferred_element_type=F32)
        total = cum[L - 1:L, :]
        rev = total - cum + lf
        gi = pltpu.roll(gh, 1, 1)
        bc_ref[pl.ds(c0, L), :] = jnp.where(lane == 3, rev, cum)
        ac_ref[pl.ds(c0, L), :] = jnp.where(lane == 3, cum - lf, total - cum) + gi
        gr = grow_ref[0, 0, :, pl.ds(c0, L)]
        lfr = _log_sigmoid(gr)
        hr, lr = _split_bf16(lfr)
        st = jnp.concatenate([hr, lr], axis=0)
        pre = jnp.dot(st, triu_b, preferred_element_type=F32)
        suf = jnp.dot(st, tril_b, preferred_element_type=F32)
        pre = pre[0:GATE_SLOTS] + pre[GATE_SLOTS:]
        suf = suf[0:GATE_SLOTS] + suf[GATE_SLOTS:]
        rows_ref[0:1, pl.ds(c0, L)] = gr[0:1] - pre[1:2]
        rows_ref[1:2, pl.ds(c0, L)] = gr[2:3] - suf[3:4]
        return carry

    lax.fori_loop(0, nc, gate_chunk, 0)

    lower = ci <= ri
    upper = ri <= ci
    ones_col = (lax.broadcasted_iota(jnp.int32, (L, LANES), 1) == 0).astype(BF16)
    ct_ref[...] = jnp.zeros_like(ct_ref)

    def direction(c, m, d_idx, mask, lane_idx, g_row, out_ref):
        c0 = pl.multiple_of(c * L, L)
        qc = qs_ref[pl.ds(c0, L), :]
        kc = ks_ref[pl.ds(c0, L), :]
        vc = v_ref[0, pl.ds(c0, L), :]
        vext = jnp.concatenate([vc, ones_col], axis=1)
        bblk = bc_ref[pl.ds(c0, L), :]
        b_col = bblk[:, lane_idx:lane_idx + 1]
        a_col = ac_ref[pl.ds(c0, L), :][:, lane_idx:lane_idx + 1]
        r_row = rows_ref[d_idx:d_idx + 1, pl.ds(c0, L)]
        g = b_col[g_row:g_row + 1, :]
        dmat = jnp.where(mask, b_col + r_row, -jnp.inf)
        m_inter = b_col + m
        m_t = jnp.maximum(m_inter, jnp.max(dmat, axis=1, keepdims=True))
        p = jnp.exp(dmat - m_t)
        w_inter = jnp.exp(m_inter - m_t)
        s = (_nt_dot(qc, kc) * p).astype(BF16)
        ct = ct_ref[d_idx]
        nd = w_inter * jnp.dot(qc, ct.astype(BF16), preferred_element_type=F32) \
            + jnp.dot(s, vext, preferred_element_type=F32)
        num = nd[:, :Dh]
        den = nd[:, Dh:Dh + 1]
        out_ref[pl.ds(c0, L), :] = num / jnp.maximum(jnp.abs(den), jnp.exp(-m_t))
        m_new = jnp.maximum(g + m, jnp.max(a_col, axis=0, keepdims=True))
        decay = jnp.exp(g + m - m_new)
        w_col = jnp.exp(a_col - m_new)
        wv = (w_col * vext.astype(F32)).astype(BF16)
        ct_ref[d_idx] = decay * ct + _tn_dot(kc, wv)
        return m_new

    def step(j, carry):
        m_f, m_b = carry
        m_f = direction(j, m_f, 0, lower, 1, L - 1, hf_ref)
        m_b = direction(nc - 1 - j, m_b, 1, upper, 3, 0, hb_ref)
        return m_f, m_b

    zero11 = jnp.zeros((1, 1), F32)
    lax.fori_loop(0, nc, step, (zero11, zero11))

    ng = ng_ref[...]
    for r in range(S // RB):
        sl = slice(r * RB, (r + 1) * RB)
        hsum = hf_ref[sl] + hb_ref[sl]
        hn = hsum * lax.rsqrt(jnp.mean(hsum * hsum, axis=-1, keepdims=True) + EPS)
        hn = hn * ng
        y_ref[0, sl] = (hn * jax.nn.sigmoid(o_ref[0, sl])).astype(BF16)


def _mlstm(qkm, vm, om, gcol, grow, conv_w, conv_b, norm_g):
    B, S, _ = vm.shape
    H = MLSTM_HEADS
    Dh = MLSTM_HEAD_DIM
    seq = lambda off: pl.BlockSpec((1, S, Dh), lambda b, h: (b, 0, h + off))
    return pl.pallas_call(
        _mlstm_kernel,
        grid=(B, H),
        in_specs=[
            seq(0), seq(H), seq(0), seq(0),
            pl.BlockSpec((1, S, LANES), lambda b, h: (b, 0, 0)),
            pl.BlockSpec((1, 1, GATE_SLOTS, S), lambda b, h: (b, h, 0, 0)),
            pl.BlockSpec((CONV_WIDTH, Dh), lambda b, h: (0, h)),
            pl.BlockSpec((CONV_WIDTH, Dh), lambda b, h: (0, h + H)),
            pl.BlockSpec((1, Dh), lambda b, h: (0, h)),
            pl.BlockSpec((1, Dh), lambda b, h: (0, h + H)),
            pl.BlockSpec((1, Dh), lambda b, h: (0, h)),
        ],
        out_specs=pl.BlockSpec((1, S, Dh), lambda b, h: (b, 0, h)),
        out_shape=jax.ShapeDtypeStruct((B, S, MLSTM_WIDTH), BF16),
        scratch_shapes=[
            pltpu.VMEM((S + 16, LANES), F32),
            pltpu.VMEM((S, Dh), BF16),
            pltpu.VMEM((S, Dh), BF16),
            pltpu.VMEM((S, LANES), F32),
            pltpu.VMEM((S, LANES), F32),
            pltpu.VMEM((GATE_SLOTS, S), F32),
            pltpu.VMEM((S, Dh), F32),
            pltpu.VMEM((S, Dh), F32),
            pltpu.VMEM((2, Dh, 2 * Dh), F32),
        ],
        compiler_params=pltpu.CompilerParams(
            dimension_semantics=("parallel", "parallel"), vmem_limit_bytes=VMEM_LIMIT),
        name="mlstm",
    )(qkm, qkm, vm, om, gcol, grow, conv_w, conv_w, conv_b.reshape(1, -1), conv_b.reshape(1, -1),
      norm_g.reshape(1, -1))


def _outffn_kernel(x_ref, mod_ref, yat_ref, ym_ref, wo_ref, gf_ref, wg_ref, wu_ref, wd_ref, fg_ref, o_ref):
    D = D_MODEL
    x = x_ref[0]
    mod = mod_ref[0]
    g1 = mod[:, 2 * D:3 * D]
    sh2 = mod[:, 3 * D:4 * D]
    sc2 = mod[:, 4 * D:5 * D]
    g2 = mod[:, 5 * D:6 * D]
    mix = _tn_dot(yat_ref[0], wo_ref[0:ATTN_WIDTH, :]) \
        + jnp.dot(ym_ref[0], wo_ref[ATTN_WIDTH:MIX_WIDTH, :], preferred_element_type=F32)
    x1 = x + g1 * mix
    ms = jnp.mean(x1 * x1, axis=-1, keepdims=True)
    h = (x1 * lax.rsqrt(ms + EPS)) * gf_ref[...]
    hb = (h * (1.0 + sc2) + sh2).astype(BF16)
    gate = jnp.dot(hb, wg_ref[...], preferred_element_type=F32)
    up = jnp.dot(hb, wu_ref[...], preferred_element_type=F32)
    act = (gate * jax.nn.sigmoid(gate) * up).astype(BF16)
    ffn = jnp.dot(act, wd_ref[...], preferred_element_type=F32)
    x2 = x1 + g2 * ffn
    ms2 = jnp.mean(x2 * x2, axis=-1, keepdims=True)
    o_ref[0] = (x2 * lax.rsqrt(ms2 + EPS)) * fg_ref[...]


def _outffn(x, mod3, yat, ym, w_out, norm_ffn_g, w_gate, w_up, w_down, final_g):
    B, S, D = x.shape
    tm = TM_FFN
    F = w_gate.shape[1]
    const = lambda shape: pl.BlockSpec(shape, lambda b, i: (0,) * len(shape), pipeline_mode=pl.Buffered(1))
    return pl.pallas_call(
        _outffn_kernel,
        grid=(B, S // tm),
        in_specs=[
            pl.BlockSpec((1, tm, D), lambda b, i: (b, i, 0)),
            pl.BlockSpec((1, 1, 6 * D), lambda b, i: (b, 0, 0)),
            pl.BlockSpec((1, ATTN_WIDTH, tm), lambda b, i: (b, 0, i)),
            pl.BlockSpec((1, tm, MLSTM_WIDTH), lambda b, i: (b, i, 0)),
            const((MIX_WIDTH, D)),
            const((1, D)),
            const((D, F)),
            const((D, F)),
            const((F, D)),
            const((1, D)),
        ],
        out_specs=pl.BlockSpec((1, tm, D), lambda b, i: (b, i, 0)),
        out_shape=jax.ShapeDtypeStruct((B, S, D), F32),
        compiler_params=pltpu.CompilerParams(
            dimension_semantics=("parallel", "parallel"), vmem_limit_bytes=VMEM_LIMIT),
        name="outffn",
    )(x, mod3, yat, ym, w_out.astype(BF16), norm_ffn_g.reshape(1, D), w_gate.astype(BF16),
      w_up.astype(BF16), w_down.astype(BF16), final_g.reshape(1, D))


def kernel(x, c, w_ada, b_ada, norm_mix_g, w_in, q_norm_g, k_norm_g, conv_w, conv_b, gate_b,
           mlstm_norm_g, w_out, norm_ffn_g, w_gate, w_up, w_down, final_norm_g):
    assert w_ada.shape[0] == 1, "the final RMSNorm is fused into the single layer's last kernel"
    B = x.shape[0]
    l = 0
    mod3 = _ada(c, w_ada[l], b_ada[l]).reshape(B, 1, -1)
    q, k, vt, qkm, vm, om, gcol, grow = _inproj(x, mod3, norm_mix_g[l], w_in[l], gate_b[l],
                                                q_norm_g[l], k_norm_g[l])
    yat = _attention(q, k, vt)
    ym = _mlstm(qkm, vm, om, gcol, grow, conv_w[l], conv_b[l], mlstm_norm_g[l])
    return _outffn(x, mod3, yat, ym, w_out[l], norm_ffn_g[l], w_gate[l], w_up[l], w_down[l], final_norm_g)
```

```python
import functools

import jax
import jax.numpy as jnp
import numpy as np
from jax import lax
from jax.experimental import pallas as pl
from jax.experimental.pallas import tpu as pltpu

F32 = jnp.float32
BF16 = jnp.bfloat16

D_MODEL = 1024
GRID_W = 64
EPS = 1e-6
ATTN_HEADS = 8
ATTN_KV_HEADS = 2
ATTN_GROUP = ATTN_HEADS // ATTN_KV_HEADS
ATTN_HEAD_DIM = 64
ATTN_WIDTH = ATTN_HEADS * ATTN_HEAD_DIM
KV_WIDTH = ATTN_KV_HEADS * ATTN_HEAD_DIM
ROPE_THETA = 10000.0
MLSTM_HEADS = 4
MLSTM_HEAD_DIM = 128
MLSTM_WIDTH = MLSTM_HEADS * MLSTM_HEAD_DIM
CONV_WIDTH = 5
N_GATES = 4 * MLSTM_HEADS
MIX_WIDTH = ATTN_WIDTH + MLSTM_WIDTH
FFN_HIDDEN = 256 * ((8 * D_MODEL + 3 * 256 - 1) // (3 * 256))

LANES = 128
GATE_SLOTS = 8

TM_IN = 512
TM_FFN = 512
TQ = 128
TK = 256
BF16_SUBLANES = 16
VX_ROWS = ATTN_HEAD_DIM + BF16_SUBLANES
CHUNK = 128
Q_SCALE = ATTN_HEAD_DIM ** -0.5 * float(np.log2(np.e))
VMEM_LIMIT = 56 * 1024 * 1024


def _nt_dot(a, b):
    return lax.dot_general(a, b, (((1,), (1,)), ((), ())), preferred_element_type=F32)


def _tn_dot(a, b):
    return lax.dot_general(a, b, (((0,), (0,)), ((), ())), preferred_element_type=F32)


def _ada_kernel(c_ref, w_ref, b_ref, o_ref):
    c = c_ref[...]
    cond = c * jax.nn.sigmoid(c)
    o_ref[...] = jnp.dot(cond, w_ref[...], preferred_element_type=F32,
                         precision=lax.Precision.HIGHEST) + b_ref[...]


def _ada(c, w_ada, b_ada):
    B, D = c.shape
    n = w_ada.shape[1] // D
    return pl.pallas_call(
        _ada_kernel,
        grid=(n,),
        in_specs=[pl.BlockSpec((B, D), lambda j: (0, 0)),
                  pl.BlockSpec((D, D), lambda j: (0, j)),
                  pl.BlockSpec((1, D), lambda j: (0, j))],
        out_specs=pl.BlockSpec((B, D), lambda j: (0, j)),
        out_shape=jax.ShapeDtypeStruct((B, n * D), F32),
        name="ada",
    )(c, w_ada, b_ada.reshape(1, -1))


_C_Q = 0
_C_K = _C_Q + ATTN_WIDTH
_C_V = _C_K + KV_WIDTH
_C_QKM = _C_V + KV_WIDTH
_C_VM = _C_QKM + 2 * MLSTM_WIDTH
_C_OM = _C_VM + MLSTM_WIDTH
_C_G = _C_OM + MLSTM_WIDTH
_C_END = _C_G + LANES


def _inproj_kernel(x_ref, mod_ref, g_ref, w_ref, gb_ref, qg_ref, kg_ref, cos_ref, sa_ref, sb_ref, bd_ref,
                   q_ref, k_ref, vt_ref, qkm_ref, vm_ref, om_ref, gcol_ref, grow_ref):
    D = D_MODEL
    x = x_ref[0]
    ms = jnp.mean(x * x, axis=-1, keepdims=True)
    mod = mod_ref[0]
    sh1 = mod[:, 0:D]
    sc1 = mod[:, D:2 * D]
    h = (x * lax.rsqrt(ms + EPS)) * g_ref[...]
    h = h * (1.0 + sc1) + sh1
    hb = h.astype(BF16)

    cosf = cos_ref[...]
    sa = sa_ref[...]
    sb = sb_ref[...]
    bd = bd_ref[...]

    def norm_rope(xs, gain, scale):
        ss = jnp.dot((xs * xs).astype(BF16), bd, preferred_element_type=F32)
        xn = (xs * lax.rsqrt(ss + EPS)) * gain
        out = xn * cosf + pltpu.roll(xn, LANES - 16, 1) * sa + pltpu.roll(xn, 16, 1) * sb
        return out * scale

    hd = ATTN_HEAD_DIM
    for j in range(ATTN_WIDTH // LANES):
        qs = jnp.dot(hb, w_ref[:, _C_Q + j * LANES:_C_Q + (j + 1) * LANES], preferred_element_type=F32)
        qs = norm_rope(qs, qg_ref[...], Q_SCALE).astype(BF16)
        q_ref[0, 2 * j] = qs[:, :hd]
        q_ref[0, 2 * j + 1] = qs[:, hd:]
    ks = jnp.dot(hb, w_ref[:, _C_K:_C_K + LANES], preferred_element_type=F32)
    ks = norm_rope(ks, kg_ref[...], 1.0).astype(BF16)
    k_ref[0, 0] = ks[:, :hd]
    k_ref[0, 1] = ks[:, hd:]
    vs = jnp.dot(hb, w_ref[:, _C_V:_C_V + LANES], preferred_element_type=F32)
    vt = vs.T.astype(BF16)
    vt_ref[0, 0] = vt[:hd]
    vt_ref[0, 1] = vt[hd:]
    qkm_ref[0] = jnp.dot(hb, w_ref[:, _C_QKM:_C_VM], preferred_element_type=F32)
    vm_ref[0] = jnp.dot(hb, w_ref[:, _C_VM:_C_OM], preferred_element_type=F32).astype(BF16)
    om_ref[0] = jnp.dot(hb, w_ref[:, _C_OM:_C_G], preferred_element_type=F32)
    gates = jnp.dot(hb, w_ref[:, _C_G:_C_END], preferred_element_type=F32) + gb_ref[...]
    gcol_ref[0] = gates
    gt = gates.T
    tm = gates.shape[0]
    grow_ref[0] = gt[:MLSTM_HEADS * GATE_SLOTS].reshape(MLSTM_HEADS, GATE_SLOTS, tm)


def _gate_perm():
    src = np.zeros((MLSTM_HEADS, 4), np.int32)
    for h in range(MLSTM_HEADS):
        for t in range(4):
            src[h, t] = t * MLSTM_HEADS + h
    return src


def _rope_tables(S):
    rows = S // GRID_W
    row = jnp.repeat(jnp.arange(rows, dtype=F32), GRID_W)
    col = jnp.tile(jnp.arange(GRID_W, dtype=F32), rows)
    axis_dim = ATTN_HEAD_DIM // 2
    inv_freq = ROPE_THETA ** (-jnp.arange(0, axis_dim, 2, dtype=F32) / axis_dim)
    ar = row[:, None] * inv_freq[None, :]
    ac = col[:, None] * inv_freq[None, :]
    cr, sr, cc, sc = jnp.cos(ar), jnp.sin(ar), jnp.cos(ac), jnp.sin(ac)
    z = jnp.zeros_like(sr)
    cos_h = jnp.concatenate([cr, cr, cc, cc], axis=-1)
    sa_h = jnp.concatenate([-sr, z, -sc, z], axis=-1)
    sb_h = jnp.concatenate([z, sr, z, sc], axis=-1)
    two = lambda a: jnp.concatenate([a, a], axis=-1)
    return two(cos_h), two(sa_h), two(sb_h)


def _inproj(x, mod3, norm_g, w_in, gate_b, q_g, k_g):
    B, S, D = x.shape
    tm = TM_IN
    H = MLSTM_HEADS
    perm = _gate_perm()
    wg = jnp.zeros((D, LANES), F32)
    gb = jnp.zeros((1, LANES), F32)
    gate0 = IN_PROJ_GATE0
    for h in range(H):
        wg = wg.at[:, GATE_SLOTS * h:GATE_SLOTS * h + 4].set(w_in[:, gate0 + perm[h]])
        gb = gb.at[0, GATE_SLOTS * h:GATE_SLOTS * h + 4].set(gate_b[perm[h]])
    w = jnp.concatenate([w_in[:, :gate0], wg], axis=1).astype(BF16)
    cosf, sa, sb = _rope_tables(S)
    qg2 = jnp.concatenate([q_g, q_g]).reshape(1, LANES)
    kg2 = jnp.concatenate([k_g, k_g]).reshape(1, LANES)
    blk = np.kron(np.eye(LANES // ATTN_HEAD_DIM), np.ones((ATTN_HEAD_DIM, ATTN_HEAD_DIM))) / ATTN_HEAD_DIM
    bd = jnp.asarray(blk, BF16)

    const = lambda shape: pl.BlockSpec(shape, lambda b, i: (0,) * len(shape))
    hd = ATTN_HEAD_DIM
    out_shape = (
        jax.ShapeDtypeStruct((B, ATTN_HEADS, S, hd), BF16),
        jax.ShapeDtypeStruct((B, ATTN_KV_HEADS, S, hd), BF16),
        jax.ShapeDtypeStruct((B, ATTN_KV_HEADS, hd, S), BF16),
        jax.ShapeDtypeStruct((B, S, 2 * MLSTM_WIDTH), F32),
        jax.ShapeDtypeStruct((B, S, MLSTM_WIDTH), BF16),
        jax.ShapeDtypeStruct((B, S, MLSTM_WIDTH), F32),
        jax.ShapeDtypeStruct((B, S, LANES), F32),
        jax.ShapeDtypeStruct((B, H, GATE_SLOTS, S), F32),
    )
    out_specs = (
        pl.BlockSpec((1, ATTN_HEADS, tm, hd), lambda b, i: (b, 0, i, 0)),
        pl.BlockSpec((1, ATTN_KV_HEADS, tm, hd), lambda b, i: (b, 0, i, 0)),
        pl.BlockSpec((1, ATTN_KV_HEADS, hd, tm), lambda b, i: (b, 0, 0, i)),
        pl.BlockSpec((1, tm, 2 * MLSTM_WIDTH), lambda b, i: (b, i, 0)),
        pl.BlockSpec((1, tm, MLSTM_WIDTH), lambda b, i: (b, i, 0)),
        pl.BlockSpec((1, tm, MLSTM_WIDTH), lambda b, i: (b, i, 0)),
        pl.BlockSpec((1, tm, LANES), lambda b, i: (b, i, 0)),
        pl.BlockSpec((1, H, GATE_SLOTS, tm), lambda b, i: (b, 0, 0, i)),
    )
    in_specs = [
        pl.BlockSpec((1, tm, D), lambda b, i: (b, i, 0)),
        pl.BlockSpec((1, 1, 6 * D), lambda b, i: (b, 0, 0)),
        const((1, D)),
        const((D, _C_END)),
        const((1, LANES)),
        const((1, LANES)),
        const((1, LANES)),
        pl.BlockSpec((tm, LANES), lambda b, i: (i, 0)),
        pl.BlockSpec((tm, LANES), lambda b, i: (i, 0)),
        pl.BlockSpec((tm, LANES), lambda b, i: (i, 0)),
        const((LANES, LANES)),
    ]
    return pl.pallas_call(
        _inproj_kernel,
        grid=(B, S // tm),
        in_specs=in_specs,
        out_specs=out_specs,
        out_shape=out_shape,
        compiler_params=pltpu.CompilerParams(
            dimension_semantics=("parallel", "parallel"), vmem_limit_bytes=VMEM_LIMIT),
        name="inproj",
    )(x, mod3, norm_g.reshape(1, D), w, gb, qg2, kg2, cosf, sa, sb, bd)


IN_PROJ_GATE0 = ATTN_WIDTH + 2 * KV_WIDTH + 4 * MLSTM_WIDTH


def _attn_kernel(q_ref, k_ref, vt_ref, o_ref, s0_ref, vx_ref):
    G = ATTN_GROUP
    hd = ATTN_HEAD_DIM
    S = k_ref.shape[2]
    R = G * TQ
    nk = S // TK
    nq = S // TQ

    vx_ref[0:hd] = vt_ref[0, 0]
    extra = lax.broadcasted_iota(jnp.int32, (VX_ROWS - hd, S), 0)
    vx_ref[hd:VX_ROWS] = (extra == 0).astype(BF16)

    def load_q(qi):
        r0 = pl.multiple_of(qi * TQ, TQ)
        return jnp.concatenate([q_ref[0, g, pl.ds(r0, TQ), :] for g in range(G)], axis=0)

    def scores(qb, kj):
        kb = k_ref[0, 0, kj * TK:(kj + 1) * TK, :]
        return _nt_dot(kb, qb)

    def update(kj, s, m, acc):
        mn = jnp.maximum(m, jnp.max(s, axis=0, keepdims=True))
        alpha = jnp.exp2(m - mn)
        p = jnp.exp2(s - mn).astype(BF16)
        vb = vx_ref[:, kj * TK:(kj + 1) * TK]
        acc = alpha * acc + jnp.dot(vb, p, preferred_element_type=F32)
        return mn, acc

    s0_ref[...] = scores(load_q(0), 0)

    def qblock(qi, carry):
        qb = load_q(qi)
        ss = [scores(qb, kj) for kj in range(1, nk)]
        s_next = scores(load_q(jnp.minimum(qi + 1, nq - 1)), 0)
        m = jnp.full((1, R), -jnp.inf, F32)
        acc = jnp.zeros((VX_ROWS, R), F32)
        m, acc = update(0, s0_ref[...], m, acc)
        for kj in range(1, nk):
            m, acc = update(kj, ss[kj - 1], m, acc)
        s0_ref[...] = s_next
        o = (acc[0:hd] / acc[hd:hd + 1]).astype(BF16)
        r0 = pl.multiple_of(qi * TQ, TQ)
        for g in range(G):
            o_ref[0, g * hd:(g + 1) * hd, pl.ds(r0, TQ)] = o[:, g * TQ:(g + 1) * TQ]
        return carry

    lax.fori_loop(0, nq, qblock, 0)


def _attention(q, k, vt):
    B, _, S, hd = q.shape
    G = ATTN_GROUP
    return pl.pallas_call(
        _attn_kernel,
        grid=(B, ATTN_KV_HEADS),
        in_specs=[
            pl.BlockSpec((1, G, S, hd), lambda b, kv: (b, kv, 0, 0)),
            pl.BlockSpec((1, 1, S, hd), lambda b, kv: (b, kv, 0, 0)),
            pl.BlockSpec((1, 1, hd, S), lambda b, kv: (b, kv, 0, 0)),
        ],
        out_specs=pl.BlockSpec((1, G * hd, S), lambda b, kv: (b, kv, 0)),
        out_shape=jax.ShapeDtypeStruct((B, ATTN_WIDTH, S), BF16),
        scratch_shapes=[pltpu.VMEM((TK, G * TQ), F32),
                        pltpu.VMEM((VX_ROWS, S), BF16)],
        compiler_params=pltpu.CompilerParams(
            dimension_semantics=("parallel", "parallel"), vmem_limit_bytes=VMEM_LIMIT),
        name="attn",
    )(q, k, vt)


def _log_sigmoid(x):
    return jnp.minimum(x, 0.0) - jnp.log(1.0 + jnp.exp(-jnp.abs(x)))


def _split_bf16(x):
    hi = x.astype(BF16)
    lo = (x - hi.astype(F32)).astype(BF16)
    return hi, lo


def _mlstm_kernel(qpre_ref, kpre_ref, v_ref, o_ref, gcol_ref, grow_ref, cwq_ref, cwk_ref, cbq_ref, cbk_ref,
                  ng_ref, y_ref, xp_ref, qs_ref, ks_ref, bc_ref, ac_ref, rows_ref, hf_ref, hb_ref, ct_ref):
    S = v_ref.shape[1]
    L = CHUNK
    nc = S // L
    Dh = MLSTM_HEAD_DIM
    head = pl.program_id(1)

    pad = 8
    zeros_pad = jnp.zeros((pad, LANES), F32)
    xp_ref[0:pad] = zeros_pad
    xp_ref[pad + S:pad + S + pad] = zeros_pad
    RB = 512
    for src_ref, cw_ref, cb_ref, dst_ref, scale in (
            (qpre_ref, cwq_ref, cbq_ref, qs_ref, 1.0),
            (kpre_ref, cwk_ref, cbk_ref, ks_ref, Dh ** -0.5)):
        xp_ref[pad:pad + S] = src_ref[0]
        cw = cw_ref[...]
        cb = cb_ref[...]
        for r in range(S // RB):
            acc = jnp.zeros((RB, LANES), F32) + cb
            for j in range(CONV_WIDTH):
                off = pad + r * RB + j - CONV_WIDTH // 2
                acc = acc + xp_ref[off:off + RB] * cw[j:j + 1]
            y = acc * jax.nn.sigmoid(acc)
            dst_ref[r * RB:(r + 1) * RB] = (y * scale).astype(BF16)

    lane = lax.broadcasted_iota(jnp.int32, (L, LANES), 1)
    ri = lax.broadcasted_iota(jnp.int32, (L, L), 0)
    ci = lax.broadcasted_iota(jnp.int32, (L, L), 1)
    tril_b = (ci <= ri).astype(BF16)
    triu_b = (ri <= ci).astype(BF16)
    shift = (LANES - GATE_SLOTS * head) % LANES

    def gate_chunk(c, carry):
        c0 = pl.multiple_of(c * L, L)
        gh = pltpu.roll(gcol_ref[0, pl.ds(c0, L), :], shift, 1)
        lf = _log_sigmoid(gh)
        hi, lo = _split_bf16(lf)
        cum = jnp.dot(tril_b, hi, preferred_element_type=F32) + jnp.dot(tril_b, lo, preferred_element_type=F32)
        total = cum[L - 1:L, :]
        rev = total - cum + lf
        gi = pltpu.roll(gh, 1, 1)
        bc_ref[pl.ds(c0, L), :] = jnp.where(lane == 3, rev, cum)
        ac_ref[pl.ds(c0, L), :] = jnp.where(lane == 3, cum - lf, total - cum) + gi
        gr = grow_ref[0, 0, :, pl.ds(c0, L)]
        lfr = _log_sigmoid(gr)
        hr, lr = _split_bf16(lfr)
        st = jnp.concatenate([hr, lr], axis=0)
        pre = jnp.dot(st, triu_b, preferred_element_type=F32)
        suf = jnp.dot(st, tril_b, preferred_element_type=F32)
        pre = pre[0:GATE_SLOTS] + pre[GATE_SLOTS:]
        suf = suf[0:GATE_SLOTS] + suf[GATE_SLOTS:]
        rows_ref[0:1, pl.ds(c0, L)] = gr[0:1] - pre[1:2]
        rows_ref[1:2, pl.ds(c0, L)] = gr[2:3] - suf[3:4]
        return carry

    lax.fori_loop(0, nc, gate_chunk, 0)

    lower = ci <= ri
    upper = ri <= ci
    ones_col = (lax.broadcasted_iota(jnp.int32, (L, LANES), 1) == 0).astype(BF16)
    ct_ref[...] = jnp.zeros_like(ct_ref)

    def direction(c, m, d_idx, mask, lane_idx, g_row, out_ref):
        c0 = pl.multiple_of(c * L, L)
        qc = qs_ref[pl.ds(c0, L), :]
        kc = ks_ref[pl.ds(c0, L), :]
        vc = v_ref[0, pl.ds(c0, L), :]
        vext = jnp.concatenate([vc, ones_col], axis=1)
        bblk = bc_ref[pl.ds(c0, L), :]
        b_col = bblk[:, lane_idx:lane_idx + 1]
        a_col = ac_ref[pl.ds(c0, L), :][:, lane_idx:lane_idx + 1]
        r_row = rows_ref[d_idx:d_idx + 1, pl.ds(c0, L)]
        g = b_col[g_row:g_row + 1, :]
        dmat = jnp.where(mask, b_col + r_row, -jnp.inf)
        m_inter = b_col + m
        m_t = jnp.maximum(m_inter, jnp.max(dmat, axis=1, keepdims=True))
        p = jnp.exp(dmat - m_t)
        w_inter = jnp.exp(m_inter - m_t)
        s = (_nt_dot(qc, kc) * p).astype(BF16)
        ct = ct_ref[d_idx]
        nd = w_inter * jnp.dot(qc, ct.astype(BF16), preferred_element_type=F32) \
            + jnp.dot(s, vext, preferred_element_type=F32)
        num = nd[:, :Dh]
        den = nd[:, Dh:Dh + 1]
        out_ref[pl.ds(c0, L), :] = num / jnp.maximum(jnp.abs(den), jnp.exp(-m_t))
        m_new = jnp.maximum(g + m, jnp.max(a_col, axis=0, keepdims=True))
        decay = jnp.exp(g + m - m_new)
        w_col = jnp.exp(a_col - m_new)
        wv = (w_col * vext.astype(F32)).astype(BF16)
        ct_ref[d_idx] = decay * ct + _tn_dot(kc, wv)
        return m_new

    def step(j, carry):
        m_f, m_b = carry
        m_f = direction(j, m_f, 0, lower, 1, L - 1, hf_ref)
        m_b = direction(nc - 1 - j, m_b, 1, upper, 3, 0, hb_ref)
        return m_f, m_b

    zero11 = jnp.zeros((1, 1), F32)
    lax.fori_loop(0, nc, step, (zero11, zero11))

    ng = ng_ref[...]
    for r in range(S // RB):
        sl = slice(r * RB, (r + 1) * RB)
        hsum = hf_ref[sl] + hb_ref[sl]
        hn = hsum * lax.rsqrt(jnp.mean(hsum * hsum, axis=-1, keepdims=True) + EPS)
        hn = hn * ng
        y_ref[0, sl] = (hn * jax.nn.sigmoid(o_ref[0, sl])).astype(BF16)


def _mlstm(qkm, vm, om, gcol, grow, conv_w, conv_b, norm_g):
    B, S, _ = vm.shape
    H = MLSTM_HEADS
    Dh = MLSTM_HEAD_DIM
    seq = lambda off: pl.BlockSpec((1, S, Dh), lambda b, h: (b, 0, h + off))
    return pl.pallas_call(
        _mlstm_kernel,
        grid=(B, H),
        in_specs=[
            seq(0), seq(H), seq(0), seq(0),
            pl.BlockSpec((1, S, LANES), lambda b, h: (b, 0, 0)),
            pl.BlockSpec((1, 1, GATE_SLOTS, S), lambda b, h: (b, h, 0, 0)),
            pl.BlockSpec((CONV_WIDTH, Dh), lambda b, h: (0, h)),
            pl.BlockSpec((CONV_WIDTH, Dh), lambda b, h: (0, h + H)),
            pl.BlockSpec((1, Dh), lambda b, h: (0, h)),
            pl.BlockSpec((1, Dh), lambda b, h: (0, h + H)),
            pl.BlockSpec((1, Dh), lambda b, h: (0, h)),
        ],
        out_specs=pl.BlockSpec((1, S, Dh), lambda b, h: (b, 0, h)),
        out_shape=jax.ShapeDtypeStruct((B, S, MLSTM_WIDTH), BF16),
        scratch_shapes=[
            pltpu.VMEM((S + 16, LANES), F32),
            pltpu.VMEM((S, Dh), BF16),
            pltpu.VMEM((S, Dh), BF16),
            pltpu.VMEM((S, LANES), F32),
            pltpu.VMEM((S, LANES), F32),
            pltpu.VMEM((GATE_SLOTS, S), F32),
            pltpu.VMEM((S, Dh), F32),
            pltpu.VMEM((S, Dh), F32),
            pltpu.VMEM((2, Dh, 2 * Dh), F32),
        ],
        compiler_params=pltpu.CompilerParams(
            dimension_semantics=("parallel", "parallel"), vmem_limit_bytes=VMEM_LIMIT),
        name="mlstm",
    )(qkm, qkm, vm, om, gcol, grow, conv_w, conv_w, conv_b.reshape(1, -1), conv_b.reshape(1, -1),
      norm_g.reshape(1, -1))


def _outffn_kernel(x_ref, mod_ref, yat_ref, ym_ref, wo_ref, gf_ref, wg_ref, wu_ref, wd_ref, fg_ref, o_ref):
    D = D_MODEL
    x = x_ref[0]
    mod = mod_ref[0]
    g1 = mod[:, 2 * D:3 * D]
    sh2 = mod[:, 3 * D:4 * D]
    sc2 = mod[:, 4 * D:5 * D]
    g2 = mod[:, 5 * D:6 * D]
    mix = _tn_dot(yat_ref[0], wo_ref[0:ATTN_WIDTH, :]) \
        + jnp.dot(ym_ref[0], wo_ref[ATTN_WIDTH:MIX_WIDTH, :], preferred_element_type=F32)
    x1 = x + g1 * mix
    ms = jnp.mean(x1 * x1, axis=-1, keepdims=True)
    h = (x1 * lax.rsqrt(ms + EPS)) * gf_ref[...]
    hb = (h * (1.0 + sc2) + sh2).astype(BF16)
    gate = jnp.dot(hb, wg_ref[...], preferred_element_type=F32)
    up = jnp.dot(hb, wu_ref[...], preferred_element_type=F32)
    act = (gate * jax.nn.sigmoid(gate) * up).astype(BF16)
    ffn = jnp.dot(act, wd_ref[...], preferred_element_type=F32)
    x2 = x1 + g2 * ffn
    ms2 = jnp.mean(x2 * x2, axis=-1, keepdims=True)
    o_ref[0] = (x2 * lax.rsqrt(ms2 + EPS)) * fg_ref[...]


def _outffn(x, mod3, yat, ym, w_out, norm_ffn_g, w_gate, w_up, w_down, final_g):
    B, S, D = x.shape
    tm = TM_FFN
    F = w_gate.shape[1]
    const = lambda shape: pl.BlockSpec(shape, lambda b, i: (0,) * len(shape), pipeline_mode=pl.Buffered(1))
    return pl.pallas_call(
        _outffn_kernel,
        grid=(B, S // tm),
        in_specs=[
            pl.BlockSpec((1, tm, D), lambda b, i: (b, i, 0)),
            pl.BlockSpec((1, 1, 6 * D), lambda b, i: (b, 0, 0)),
            pl.BlockSpec((1, ATTN_WIDTH, tm), lambda b, i: (b, 0, i)),
            pl.BlockSpec((1, tm, MLSTM_WIDTH), lambda b, i: (b, i, 0)),
            const((MIX_WIDTH, D)),
            const((1, D)),
            const((D, F)),
            const((D, F)),
            const((F, D)),
            const((1, D)),
        ],
        out_specs=pl.BlockSpec((1, tm, D), lambda b, i: (b, i, 0)),
        out_shape=jax.ShapeDtypeStruct((B, S, D), F32),
        compiler_params=pltpu.CompilerParams(
            dimension_semantics=("parallel", "parallel"), vmem_limit_bytes=VMEM_LIMIT),
        name="outffn",
    )(x, mod3, yat, ym, w_out.astype(BF16), norm_ffn_g.reshape(1, D), w_gate.astype(BF16),
      w_up.astype(BF16), w_down.astype(BF16), final_g.reshape(1, D))


def kernel(x, c, w_ada, b_ada, norm_mix_g, w_in, q_norm_g, k_norm_g, conv_w, conv_b, gate_b,
           mlstm_norm_g, w_out, norm_ffn_g, w_gate, w_up, w_down, final_norm_g):
    assert w_ada.shape[0] == 1, "the final RMSNorm is fused into the single layer's last kernel"
    B = x.shape[0]
    l = 0
    mod3 = _ada(c, w_ada[l], b_ada[l]).reshape(B, 1, -1)
    q, k, vt, qkm, vm, om, gcol, grow = _inproj(x, mod3, norm_mix_g[l], w_in[l], gate_b[l],
                                                q_norm_g[l], k_norm_g[l])
    yat = _attention(q, k, vt)
    ym = _mlstm(qkm, vm, om, gcol, grow, conv_w[l], conv_b[l], mlstm_norm_g[l])
    return _outffn(x, mod3, yat, ym, w_out[l], norm_ffn_g[l], w_gate[l], w_up[l], w_down[l], final_norm_g)
```

```python
import functools

import jax
import jax.numpy as jnp
import numpy as np
from jax import lax
from jax.experimental import pallas as pl
from jax.experimental.pallas import tpu as pltpu

F32 = jnp.float32
BF16 = jnp.bfloat16

D_MODEL = 1024
GRID_W = 64
EPS = 1e-6
ATTN_HEADS = 8
ATTN_KV_HEADS = 2
ATTN_GROUP = ATTN_HEADS // ATTN_KV_HEADS
ATTN_HEAD_DIM = 64
ATTN_WIDTH = ATTN_HEADS * ATTN_HEAD_DIM
KV_WIDTH = ATTN_KV_HEADS * ATTN_HEAD_DIM
ROPE_THETA = 10000.0
MLSTM_HEADS = 4
MLSTM_HEAD_DIM = 128
MLSTM_WIDTH = MLSTM_HEADS * MLSTM_HEAD_DIM
CONV_WIDTH = 5
N_GATES = 4 * MLSTM_HEADS
MIX_WIDTH = ATTN_WIDTH + MLSTM_WIDTH
FFN_HIDDEN = 256 * ((8 * D_MODEL + 3 * 256 - 1) // (3 * 256))

LANES = 128
GATE_SLOTS = 8

TM_IN = 512
TM_FFN = 512
TQ = 128
TK = 256
F32_SUBLANES = 8
BF16_SUBLANES = 16
LOG2E = float(np.log2(np.e))
VX_ROWS = ATTN_HEAD_DIM + BF16_SUBLANES
CHUNK = 128
Q_SCALE = ATTN_HEAD_DIM ** -0.5 * float(np.log2(np.e))
VMEM_LIMIT = 56 * 1024 * 1024


def _nt_dot(a, b):
    return lax.dot_general(a, b, (((1,), (1,)), ((), ())), preferred_element_type=F32)


def _tn_dot(a, b):
    return lax.dot_general(a, b, (((0,), (0,)), ((), ())), preferred_element_type=F32)


def _ada_kernel(c_ref, w_ref, b_ref, o_ref):
    c = c_ref[...]
    cond = c * jax.nn.sigmoid(c)
    o_ref[...] = jnp.dot(cond, w_ref[...], preferred_element_type=F32,
                         precision=lax.Precision.HIGHEST) + b_ref[...]


def _ada(c, w_ada, b_ada):
    B, D = c.shape
    n = w_ada.shape[1] // D
    return pl.pallas_call(
        _ada_kernel,
        grid=(n,),
        in_specs=[pl.BlockSpec((B, D), lambda j: (0, 0)),
                  pl.BlockSpec((D, D), lambda j: (0, j)),
                  pl.BlockSpec((1, D), lambda j: (0, j))],
        out_specs=pl.BlockSpec((B, D), lambda j: (0, j)),
        out_shape=jax.ShapeDtypeStruct((B, n * D), F32),
        name="ada",
    )(c, w_ada, b_ada.reshape(1, -1))


_C_Q = 0
_C_K = _C_Q + ATTN_WIDTH
_C_V = _C_K + KV_WIDTH
_C_QKM = _C_V + KV_WIDTH
_C_VM = _C_QKM + 2 * MLSTM_WIDTH
_C_OM = _C_VM + MLSTM_WIDTH
_C_G = _C_OM + MLSTM_WIDTH
_C_END = _C_G + LANES


def _inproj_kernel(x_ref, mod_ref, g_ref, w_ref, gb_ref, qg_ref, kg_ref, cos_ref, sa_ref, sb_ref, bd_ref,
                   q_ref, k_ref, vt_ref, qkm_ref, vm_ref, om_ref, grow_ref):
    D = D_MODEL
    x = x_ref[0]
    ms = jnp.mean(x * x, axis=-1, keepdims=True)
    mod = mod_ref[0]
    sh1 = mod[:, 0:D]
    sc1 = mod[:, D:2 * D]
    h = (x * lax.rsqrt(ms + EPS)) * g_ref[...]
    h = h * (1.0 + sc1) + sh1
    hb = h.astype(BF16)

    cosf = cos_ref[...]
    sa = sa_ref[...]
    sb = sb_ref[...]
    bd = bd_ref[...]

    def norm_rope(xs, gain, scale):
        ss = jnp.dot((xs * xs).astype(BF16), bd, preferred_element_type=F32)
        xn = (xs * lax.rsqrt(ss + EPS)) * gain
        out = xn * cosf + pltpu.roll(xn, LANES - 16, 1) * sa + pltpu.roll(xn, 16, 1) * sb
        return out * scale

    hd = ATTN_HEAD_DIM
    for j in range(ATTN_WIDTH // LANES):
        qs = jnp.dot(hb, w_ref[:, _C_Q + j * LANES:_C_Q + (j + 1) * LANES], preferred_element_type=F32)
        qs = norm_rope(qs, qg_ref[...], Q_SCALE).astype(BF16)
        q_ref[0, 2 * j] = qs[:, :hd]
        q_ref[0, 2 * j + 1] = qs[:, hd:]
    ks = jnp.dot(hb, w_ref[:, _C_K:_C_K + LANES], preferred_element_type=F32)
    ks = norm_rope(ks, kg_ref[...], 1.0).astype(BF16)
    k_ref[0, 0] = ks[:, :hd]
    k_ref[0, 1] = ks[:, hd:]
    vs = jnp.dot(hb, w_ref[:, _C_V:_C_V + LANES], preferred_element_type=F32)
    vt = vs.T.astype(BF16)
    vt_ref[0, 0] = vt[:hd]
    vt_ref[0, 1] = vt[hd:]
    qkm_ref[0] = jnp.dot(hb, w_ref[:, _C_QKM:_C_VM], preferred_element_type=F32)
    vm_ref[0] = jnp.dot(hb, w_ref[:, _C_VM:_C_OM], preferred_element_type=F32).astype(BF16)
    om_ref[0] = jnp.dot(hb, w_ref[:, _C_OM:_C_G], preferred_element_type=F32)
    gates = jnp.dot(hb, w_ref[:, _C_G:_C_END], preferred_element_type=F32) + gb_ref[...]
    gt = gates.T
    tm = gates.shape[0]
    grow_ref[0] = gt[:MLSTM_HEADS * GATE_SLOTS].reshape(MLSTM_HEADS, GATE_SLOTS, tm)


def _gate_perm():
    src = np.zeros((MLSTM_HEADS, 4), np.int32)
    for h in range(MLSTM_HEADS):
        for t in range(4):
            src[h, t] = t * MLSTM_HEADS + h
    return src


def _rope_tables(S):
    rows = S // GRID_W
    row = jnp.repeat(jnp.arange(rows, dtype=F32), GRID_W)
    col = jnp.tile(jnp.arange(GRID_W, dtype=F32), rows)
    axis_dim = ATTN_HEAD_DIM // 2
    inv_freq = ROPE_THETA ** (-jnp.arange(0, axis_dim, 2, dtype=F32) / axis_dim)
    ar = row[:, None] * inv_freq[None, :]
    ac = col[:, None] * inv_freq[None, :]
    cr, sr, cc, sc = jnp.cos(ar), jnp.sin(ar), jnp.cos(ac), jnp.sin(ac)
    z = jnp.zeros_like(sr)
    cos_h = jnp.concatenate([cr, cr, cc, cc], axis=-1)
    sa_h = jnp.concatenate([-sr, z, -sc, z], axis=-1)
    sb_h = jnp.concatenate([z, sr, z, sc], axis=-1)
    two = lambda a: jnp.concatenate([a, a], axis=-1)
    return two(cos_h), two(sa_h), two(sb_h)


def _inproj(x, mod3, norm_g, w_in, gate_b, q_g, k_g):
    B, S, D = x.shape
    tm = TM_IN
    H = MLSTM_HEADS
    perm = _gate_perm()
    wg = jnp.zeros((D, LANES), F32)
    gb = jnp.zeros((1, LANES), F32)
    gate0 = IN_PROJ_GATE0
    for h in range(H):
        wg = wg.at[:, GATE_SLOTS * h:GATE_SLOTS * h + 4].set(w_in[:, gate0 + perm[h]])
        gb = gb.at[0, GATE_SLOTS * h:GATE_SLOTS * h + 4].set(gate_b[perm[h]])
    w = jnp.concatenate([w_in[:, :gate0], wg], axis=1).astype(BF16)
    cosf, sa, sb = _rope_tables(S)
    qg2 = jnp.concatenate([q_g, q_g]).reshape(1, LANES)
    kg2 = jnp.concatenate([k_g, k_g]).reshape(1, LANES)
    blk = np.kron(np.eye(LANES // ATTN_HEAD_DIM), np.ones((ATTN_HEAD_DIM, ATTN_HEAD_DIM))) / ATTN_HEAD_DIM
    bd = jnp.asarray(blk, BF16)

    const = lambda shape: pl.BlockSpec(shape, lambda b, i: (0,) * len(shape))
    hd = ATTN_HEAD_DIM
    out_shape = (
        jax.ShapeDtypeStruct((B, ATTN_HEADS, S, hd), BF16),
        jax.ShapeDtypeStruct((B, ATTN_KV_HEADS, S, hd), BF16),
        jax.ShapeDtypeStruct((B, ATTN_KV_HEADS, hd, S), BF16),
        jax.ShapeDtypeStruct((B, S, 2 * MLSTM_WIDTH), F32),
        jax.ShapeDtypeStruct((B, S, MLSTM_WIDTH), BF16),
        jax.ShapeDtypeStruct((B, S, MLSTM_WIDTH), F32),
        jax.ShapeDtypeStruct((B, H, GATE_SLOTS, S), F32),
    )
    out_specs = (
        pl.BlockSpec((1, ATTN_HEADS, tm, hd), lambda b, i: (b, 0, i, 0)),
        pl.BlockSpec((1, ATTN_KV_HEADS, tm, hd), lambda b, i: (b, 0, i, 0)),
        pl.BlockSpec((1, ATTN_KV_HEADS, hd, tm), lambda b, i: (b, 0, 0, i)),
        pl.BlockSpec((1, tm, 2 * MLSTM_WIDTH), lambda b, i: (b, i, 0)),
        pl.BlockSpec((1, tm, MLSTM_WIDTH), lambda b, i: (b, i, 0)),
        pl.BlockSpec((1, tm, MLSTM_WIDTH), lambda b, i: (b, i, 0)),
        pl.BlockSpec((1, H, GATE_SLOTS, tm), lambda b, i: (b, 0, 0, i)),
    )
    in_specs = [
        pl.BlockSpec((1, tm, D), lambda b, i: (b, i, 0)),
        pl.BlockSpec((1, 1, 6 * D), lambda b, i: (b, 0, 0)),
        const((1, D)),
        const((D, _C_END)),
        const((1, LANES)),
        const((1, LANES)),
        const((1, LANES)),
        pl.BlockSpec((tm, LANES), lambda b, i: (i, 0)),
        pl.BlockSpec((tm, LANES), lambda b, i: (i, 0)),
        pl.BlockSpec((tm, LANES), lambda b, i: (i, 0)),
        const((LANES, LANES)),
    ]
    return pl.pallas_call(
        _inproj_kernel,
        grid=(B, S // tm),
        in_specs=in_specs,
        out_specs=out_specs,
        out_shape=out_shape,
        compiler_params=pltpu.CompilerParams(
            dimension_semantics=("parallel", "parallel"), vmem_limit_bytes=VMEM_LIMIT),
        name="inproj",
    )(x, mod3, norm_g.reshape(1, D), w, gb, qg2, kg2, cosf, sa, sb, bd)


IN_PROJ_GATE0 = ATTN_WIDTH + 2 * KV_WIDTH + 4 * MLSTM_WIDTH


def _attn_kernel(q_ref, k_ref, vt_ref, o_ref, s0_ref, vx_ref):
    G = ATTN_GROUP
    hd = ATTN_HEAD_DIM
    S = k_ref.shape[2]
    R = G * TQ
    nk = S // TK
    nq = S // TQ

    vx_ref[0:hd] = vt_ref[0, 0]
    extra = lax.broadcasted_iota(jnp.int32, (VX_ROWS - hd, S), 0)
    vx_ref[hd:VX_ROWS] = (extra == 0).astype(BF16)

    def load_q(qi):
        r0 = pl.multiple_of(qi * TQ, TQ)
        return jnp.concatenate([q_ref[0, g, pl.ds(r0, TQ), :] for g in range(G)], axis=0)

    def scores(qb, kj):
        kb = k_ref[0, 0, kj * TK:(kj + 1) * TK, :]
        return _nt_dot(kb, qb)

    def update(kj, s, m, acc):
        mn = jnp.maximum(m, jnp.max(s, axis=0, keepdims=True))
        alpha = jnp.exp2(m - mn)
        p = jnp.exp2(s - mn).astype(BF16)
        vb = vx_ref[:, kj * TK:(kj + 1) * TK]
        acc = alpha * acc + jnp.dot(vb, p, preferred_element_type=F32)
        return mn, acc

    s0_ref[...] = scores(load_q(0), 0)

    def qblock(qi, carry):
        qb = load_q(qi)
        ss = [scores(qb, kj) for kj in range(1, nk)]
        s_next = scores(load_q(jnp.minimum(qi + 1, nq - 1)), 0)
        m = jnp.full((1, R), -jnp.inf, F32)
        acc = jnp.zeros((VX_ROWS, R), F32)
        m, acc = update(0, s0_ref[...], m, acc)
        for kj in range(1, nk):
            m, acc = update(kj, ss[kj - 1], m, acc)
        s0_ref[...] = s_next
        o = (acc[0:hd] / acc[hd:hd + 1]).astype(BF16)
        r0 = pl.multiple_of(qi * TQ, TQ)
        for g in range(G):
            o_ref[0, g * hd:(g + 1) * hd, pl.ds(r0, TQ)] = o[:, g * TQ:(g + 1) * TQ]
        return carry

    lax.fori_loop(0, nq, qblock, 0)


def _attention(q, k, vt):
    B, _, S, hd = q.shape
    G = ATTN_GROUP
    return pl.pallas_call(
        _attn_kernel,
        grid=(B, ATTN_KV_HEADS),
        in_specs=[
            pl.BlockSpec((1, G, S, hd), lambda b, kv: (b, kv, 0, 0)),
            pl.BlockSpec((1, 1, S, hd), lambda b, kv: (b, kv, 0, 0)),
            pl.BlockSpec((1, 1, hd, S), lambda b, kv: (b, kv, 0, 0)),
        ],
        out_specs=pl.BlockSpec((1, G * hd, S), lambda b, kv: (b, kv, 0)),
        out_shape=jax.ShapeDtypeStruct((B, ATTN_WIDTH, S), BF16),
        scratch_shapes=[pltpu.VMEM((TK, G * TQ), F32),
                        pltpu.VMEM((VX_ROWS, S), BF16)],
        compiler_params=pltpu.CompilerParams(
            dimension_semantics=("parallel", "parallel"), vmem_limit_bytes=VMEM_LIMIT),
        name="attn",
    )(q, k, vt)


def _log_sigmoid(x):
    return jnp.minimum(x, 0.0) - jnp.log(1.0 + jnp.exp(-jnp.abs(x)))


def _mlstm_kernel(qpre_ref, kpre_ref, v_ref, o_ref, grow_ref, cwq_ref, cwk_ref, cbq_ref, cbk_ref,
                  ng_ref, y_ref, xp_ref, qs_ref, ks_ref, kt_ref, rows_ref, col_ref, hf_ref, hb_ref, ct_ref):
    S = v_ref.shape[1]
    L = CHUNK
    assert L == LANES, "chunk tiles are square (L, 128) tiles"
    nc = S // L
    Dh = MLSTM_HEAD_DIM

    pad = 8
    zeros_pad = jnp.zeros((pad, LANES), F32)
    xp_ref[0:pad] = zeros_pad
    xp_ref[pad + S:pad + S + pad] = zeros_pad
    RB = 512
    for src_ref, cw_ref, cb_ref, dst_ref, scale in (
            (qpre_ref, cwq_ref, cbq_ref, qs_ref, 1.0),
            (kpre_ref, cwk_ref, cbk_ref, ks_ref, Dh ** -0.5)):
        xp_ref[pad:pad + S] = src_ref[0]
        cw = cw_ref[...]
        cb = cb_ref[...]
        for r in range(S // RB):
            acc = jnp.zeros((RB, LANES), F32) + cb
            for j in range(CONV_WIDTH):
                off = pad + r * RB + j - CONV_WIDTH // 2
                acc = acc + xp_ref[off:off + RB] * cw[j:j + 1]
            y = acc * jax.nn.sigmoid(acc) * scale
            dst_ref[r * RB:(r + 1) * RB] = y.astype(BF16)
            if dst_ref is ks_ref:
                kt_ref[:, r * RB:(r + 1) * RB] = y.T.astype(BF16)

    ri = lax.broadcasted_iota(jnp.int32, (L, L), 0)
    ci = lax.broadcasted_iota(jnp.int32, (L, L), 1)
    lower = ci <= ri
    upper = ri <= ci
    tril_b = lower.astype(BF16)
    triu_b = upper.astype(BF16)

    def split3(x):
        t1 = x.astype(BF16).astype(F32)
        e1 = x - t1
        t2 = e1.astype(BF16).astype(F32)
        t3 = (e1 - t2).astype(BF16).astype(F32)
        return t1, t2, t3

    NR = GATE_SLOTS * nc
    graw = jnp.concatenate([grow_ref[0, 0, :, c * L:(c + 1) * L] for c in range(nc)], axis=0)
    tri2 = jnp.concatenate([triu_b, tril_b], axis=1)
    ps = sum(jnp.dot(t.astype(BF16), tri2, preferred_element_type=F32)
             for t in split3(_log_sigmoid(graw) * LOG2E))
    pre, suf = ps[:, :L], ps[:, L:]
    rj = lax.broadcasted_iota(jnp.int32, (NR, L), 0) & (GATE_SLOTS - 1)
    lane = lax.broadcasted_iota(jnp.int32, (NR, L), 1)
    is_fw = rj == 0
    is_r = (rj & 5) == 0
    rr = graw * LOG2E - jnp.where(is_fw, pltpu.roll(pre, NR - 1, 0), pltpu.roll(suf, NR - 1, 0))
    x = jnp.where(is_r, rr, -jnp.inf)
    sh = 1
    while sh < L:
        fwd = jnp.where(lane >= sh, pltpu.roll(x, sh, 1), -jnp.inf)
        bwd = jnp.where(lane < L - sh, pltpu.roll(x, L - sh, 1), -jnp.inf)
        x = jnp.maximum(x, jnp.where(is_fw, fwd, bwd))
        sh *= 2
    quant = jnp.where(is_r, rr, jnp.where(rj == 1, pre, jnp.where(rj == 3, suf, jnp.where(
        (rj & 5) == 4, pltpu.roll(x, 4, 0), 0.0))))
    src_rows = (1, 0, 4, 3, 2, 6)
    ncol = len(src_rows) * LANES
    kk = lax.broadcasted_iota(jnp.int32, (4 * GATE_SLOTS, ncol), 0)
    blk = lax.shift_right_logical(lax.broadcasted_iota(jnp.int32, (4 * GATE_SLOTS, ncol), 1), 7)
    want = jnp.full((4 * GATE_SLOTS, ncol), -1, jnp.int32)
    for q, j in enumerate(src_rows):
        want = jnp.where(blk == q, j, want)
    sel = ((kk & (GATE_SLOTS - 1)) == jnp.where(kk < 3 * GATE_SLOTS, want, -1)).astype(BF16)
    terms = split3(quant)
    zeros8 = jnp.zeros((GATE_SLOTS, L), F32)
    for c in range(nc):
        rows = slice(c * GATE_SLOTS, (c + 1) * GATE_SLOTS)
        rows_ref[:, c * L:(c + 1) * L] = quant[rows]
        stacked = jnp.concatenate([t[rows] for t in terms] + [zeros8], axis=0).astype(BF16)
        col_ref[c * L:(c + 1) * L, :] = _tn_dot(stacked, sel)

    ones_blk = jnp.ones((L, LANES), BF16)
    ct_ref[...] = jnp.zeros_like(ct_ref)

    def direction(c, m, d_idx, mask, g_row, out_ref):
        c0 = pl.multiple_of(c * L, L)
        qc = qs_ref[pl.ds(c0, L), :]
        kc = ks_ref[pl.ds(c0, L), :]
        ktc = kt_ref[:, pl.ds(c0, L)]
        vc = v_ref[0, pl.ds(c0, L), :]
        r_row = rows_ref[2 * d_idx:2 * d_idx + 1, pl.ds(c0, L)]
        b = col_ref[pl.ds(c0, L), (3 * d_idx) * LANES:(3 * d_idx + 1) * LANES]
        r = col_ref[pl.ds(c0, L), (3 * d_idx + 1) * LANES:(3 * d_idx + 2) * LANES]
        cmax = col_ref[pl.ds(c0, L), (3 * d_idx + 2) * LANES:(3 * d_idx + 3) * LANES]
        rmax = cmax[g_row:g_row + 1]
        p = jnp.exp2(jnp.where(mask, r_row - cmax, -jnp.inf))
        s = (_nt_dot(qc, kc) * p).astype(BF16)
        intra = jnp.dot(s, jnp.concatenate([vc, ones_blk], axis=1), preferred_element_type=F32)
        w = jnp.exp2(r - rmax)
        wv = jnp.concatenate([(w * vc.astype(F32)).astype(BF16), w.astype(BF16)], axis=1)
        upd = jnp.dot(ktc, wv, preferred_element_type=F32)
        mc = jnp.maximum(m, cmax)
        w_inter = jnp.exp2(m - mc)
        w_intra = jnp.exp2(cmax - mc)
        ct = ct_ref[d_idx]
        inter = jnp.dot(qc, ct.astype(BF16), preferred_element_type=F32)
        num = w_inter * inter[:, :Dh] + w_intra * intra[:, :Dh]
        den = w_inter * inter[:, Dh:] + w_intra * intra[:, Dh:]
        out_ref[pl.ds(c0, L), :] = num / jnp.maximum(jnp.abs(den), jnp.exp2(-(b + mc)))
        g = b[g_row:g_row + 1]
        mu = jnp.maximum(m, rmax)
        decay = jnp.exp2(m - mu)
        gain = jnp.exp2(rmax - mu)
        ct_ref[d_idx] = jnp.concatenate([decay, decay], axis=1) * ct \
            + jnp.concatenate([gain, gain], axis=1) * upd
        return g + mu

    def step(j, carry):
        m_f, m_b = carry
        m_f = direction(j, m_f, 0, lower, L - 1, hf_ref)
        m_b = direction(nc - 1 - j, m_b, 1, upper, 0, hb_ref)
        return m_f, m_b

    zero_row = jnp.zeros((1, LANES), F32)
    lax.fori_loop(0, nc, step, (zero_row, zero_row), unroll=4)

    ng = ng_ref[...]
    for r in range(S // RB):
        sl = slice(r * RB, (r + 1) * RB)
        hsum = hf_ref[sl] + hb_ref[sl]
        hn = hsum * lax.rsqrt(jnp.mean(hsum * hsum, axis=-1, keepdims=True) + EPS)
        hn = hn * ng
        y_ref[0, sl] = (hn * jax.nn.sigmoid(o_ref[0, sl])).astype(BF16)


def _mlstm(qkm, vm, om, grow, conv_w, conv_b, norm_g):
    B, S, _ = vm.shape
    H = MLSTM_HEADS
    Dh = MLSTM_HEAD_DIM
    seq = lambda off: pl.BlockSpec((1, S, Dh), lambda b, h: (b, 0, h + off))
    return pl.pallas_call(
        _mlstm_kernel,
        grid=(B, H),
        in_specs=[
            seq(0), seq(H), seq(0), seq(0),
            pl.BlockSpec((1, 1, GATE_SLOTS, S), lambda b, h: (b, h, 0, 0)),
            pl.BlockSpec((CONV_WIDTH, Dh), lambda b, h: (0, h)),
            pl.BlockSpec((CONV_WIDTH, Dh), lambda b, h: (0, h + H)),
            pl.BlockSpec((1, Dh), lambda b, h: (0, h)),
            pl.BlockSpec((1, Dh), lambda b, h: (0, h + H)),
            pl.BlockSpec((1, Dh), lambda b, h: (0, h)),
        ],
        out_specs=pl.BlockSpec((1, S, Dh), lambda b, h: (b, 0, h)),
        out_shape=jax.ShapeDtypeStruct((B, S, MLSTM_WIDTH), BF16),
        scratch_shapes=[
            pltpu.VMEM((S + 16, LANES), F32),
            pltpu.VMEM((S, Dh), BF16),
            pltpu.VMEM((S, Dh), BF16),
            pltpu.VMEM((Dh, S), BF16),
            pltpu.VMEM((GATE_SLOTS, S), F32),
            pltpu.VMEM((S, 6 * LANES), F32),
            pltpu.VMEM((S, Dh), F32),
            pltpu.VMEM((S, Dh), F32),
            pltpu.VMEM((2, Dh, 2 * Dh), F32),
        ],
        compiler_params=pltpu.CompilerParams(
            dimension_semantics=("parallel", "parallel"), vmem_limit_bytes=VMEM_LIMIT),
        name="mlstm",
    )(qkm, qkm, vm, om, grow, conv_w, conv_w, conv_b.reshape(1, -1), conv_b.reshape(1, -1),
      norm_g.reshape(1, -1))


def _outffn_kernel(x_ref, mod_ref, yat_ref, ym_ref, wo_ref, gf_ref, wg_ref, wu_ref, wd_ref, fg_ref, o_ref):
    D = D_MODEL
    x = x_ref[0]
    mod = mod_ref[0]
    g1 = mod[:, 2 * D:3 * D]
    sh2 = mod[:, 3 * D:4 * D]
    sc2 = mod[:, 4 * D:5 * D]
    g2 = mod[:, 5 * D:6 * D]
    mix = _tn_dot(yat_ref[0], wo_ref[0:ATTN_WIDTH, :]) \
        + jnp.dot(ym_ref[0], wo_ref[ATTN_WIDTH:MIX_WIDTH, :], preferred_element_type=F32)
    x1 = x + g1 * mix
    ms = jnp.mean(x1 * x1, axis=-1, keepdims=True)
    h = (x1 * lax.rsqrt(ms + EPS)) * gf_ref[...]
    hb = (h * (1.0 + sc2) + sh2).astype(BF16)
    gate = jnp.dot(hb, wg_ref[...], preferred_element_type=F32)
    up = jnp.dot(hb, wu_ref[...], preferred_element_type=F32)
    act = (gate * jax.nn.sigmoid(gate) * up).astype(BF16)
    ffn = jnp.dot(act, wd_ref[...], preferred_element_type=F32)
    x2 = x1 + g2 * ffn
    ms2 = jnp.mean(x2 * x2, axis=-1, keepdims=True)
    o_ref[0] = (x2 * lax.rsqrt(ms2 + EPS)) * fg_ref[...]


def _outffn(x, mod3, yat, ym, w_out, norm_ffn_g, w_gate, w_up, w_down, final_g):
    B, S, D = x.shape
    tm = TM_FFN
    F = w_gate.shape[1]
    const = lambda shape: pl.BlockSpec(shape, lambda b, i: (0,) * len(shape), pipeline_mode=pl.Buffered(1))
    return pl.pallas_call(
        _outffn_kernel,
        grid=(B, S // tm),
        in_specs=[
            pl.BlockSpec((1, tm, D), lambda b, i: (b, i, 0)),
            pl.BlockSpec((1, 1, 6 * D), lambda b, i: (b, 0, 0)),
            pl.BlockSpec((1, ATTN_WIDTH, tm), lambda b, i: (b, 0, i)),
            pl.BlockSpec((1, tm, MLSTM_WIDTH), lambda b, i: (b, i, 0)),
            const((MIX_WIDTH, D)),
            const((1, D)),
            const((D, F)),
            const((D, F)),
            const((F, D)),
            const((1, D)),
        ],
        out_specs=pl.BlockSpec((1, tm, D), lambda b, i: (b, i, 0)),
        out_shape=jax.ShapeDtypeStruct((B, S, D), F32),
        compiler_params=pltpu.CompilerParams(
            dimension_semantics=("parallel", "parallel"), vmem_limit_bytes=VMEM_LIMIT),
        name="outffn",
    )(x, mod3, yat, ym, w_out.astype(BF16), norm_ffn_g.reshape(1, D), w_gate.astype(BF16),
      w_up.astype(BF16), w_down.astype(BF16), final_g.reshape(1, D))


def kernel(x, c, w_ada, b_ada, norm_mix_g, w_in, q_norm_g, k_norm_g, conv_w, conv_b, gate_b,
           mlstm_norm_g, w_out, norm_ffn_g, w_gate, w_up, w_down, final_norm_g):
    assert w_ada.shape[0] == 1, "the final RMSNorm is fused into the single layer's last kernel"
    B = x.shape[0]
    l = 0
    mod3 = _ada(c, w_ada[l], b_ada[l]).reshape(B, 1, -1)
    q, k, vt, qkm, vm, om, grow = _inproj(x, mod3, norm_mix_g[l], w_in[l], gate_b[l],
                                                q_norm_g[l], k_norm_g[l])
    yat = _attention(q, k, vt)
    ym = _mlstm(qkm, vm, om, grow, conv_w[l], conv_b[l], mlstm_norm_g[l])
    return _outffn(x, mod3, yat, ym, w_out[l], norm_ffn_g[l], w_gate[l], w_up[l], w_down[l], final_norm_g)
```

```python
import functools

import jax
import jax.numpy as jnp
import numpy as np
from jax import lax
from jax.experimental import pallas as pl
from jax.experimental.pallas import tpu as pltpu

F32 = jnp.float32
BF16 = jnp.bfloat16

D_MODEL = 1024
GRID_W = 64
EPS = 1e-6
ATTN_HEADS = 8
ATTN_KV_HEADS = 2
ATTN_GROUP = ATTN_HEADS // ATTN_KV_HEADS
ATTN_HEAD_DIM = 64
ATTN_WIDTH = ATTN_HEADS * ATTN_HEAD_DIM
KV_WIDTH = ATTN_KV_HEADS * ATTN_HEAD_DIM
ROPE_THETA = 10000.0
MLSTM_HEADS = 4
MLSTM_HEAD_DIM = 128
MLSTM_WIDTH = MLSTM_HEADS * MLSTM_HEAD_DIM
CONV_WIDTH = 5
N_GATES = 4 * MLSTM_HEADS
MIX_WIDTH = ATTN_WIDTH + MLSTM_WIDTH
FFN_HIDDEN = 256 * ((8 * D_MODEL + 3 * 256 - 1) // (3 * 256))

LANES = 128
GATE_SLOTS = 8

TM_IN = 512
TM_FFN = 512
TQ = 128
TK = 256
F32_SUBLANES = 8
BF16_SUBLANES = 16
LOG2E = float(np.log2(np.e))
VX_ROWS = ATTN_HEAD_DIM + BF16_SUBLANES
CHUNK = 128
Q_SCALE = ATTN_HEAD_DIM ** -0.5 * float(np.log2(np.e))
VMEM_LIMIT = 56 * 1024 * 1024
SAFE_SCORE_BOUND = 50.0
SCORE_TILES_AHEAD = 3
SCORE_BOUND_MARGIN = 1.02


def _nt_dot(a, b):
    return lax.dot_general(a, b, (((1,), (1,)), ((), ())), preferred_element_type=F32)


def _tn_dot(a, b):
    return lax.dot_general(a, b, (((0,), (0,)), ((), ())), preferred_element_type=F32)


def _ada_kernel(c_ref, w_ref, b_ref, o_ref):
    c = c_ref[...]
    cond = c * jax.nn.sigmoid(c)
    o_ref[...] = jnp.dot(cond, w_ref[...], preferred_element_type=F32,
                         precision=lax.Precision.HIGHEST) + b_ref[...]


def _ada(c, w_ada, b_ada):
    B, D = c.shape
    n = w_ada.shape[1] // D
    return pl.pallas_call(
        _ada_kernel,
        grid=(n,),
        in_specs=[pl.BlockSpec((B, D), lambda j: (0, 0)),
                  pl.BlockSpec((D, D), lambda j: (0, j)),
                  pl.BlockSpec((1, D), lambda j: (0, j))],
        out_specs=pl.BlockSpec((B, D), lambda j: (0, j)),
        out_shape=jax.ShapeDtypeStruct((B, n * D), F32),
        name="ada",
    )(c, w_ada, b_ada.reshape(1, -1))


_C_Q = 0
_C_K = _C_Q + ATTN_WIDTH
_C_V = _C_K + KV_WIDTH
_C_QKM = _C_V + KV_WIDTH
_C_VM = _C_QKM + 2 * MLSTM_WIDTH
_C_OM = _C_VM + MLSTM_WIDTH
_C_G = _C_OM + MLSTM_WIDTH
_C_END = _C_G + LANES


def _inproj_kernel(x_ref, mod_ref, g_ref, w_ref, gb_ref, qg_ref, kg_ref, cos_ref, sa_ref, sb_ref, bd_ref,
                   q_ref, k_ref, vt_ref, qkm_ref, vm_ref, om_ref, grow_ref):
    D = D_MODEL
    x = x_ref[0]
    ms = jnp.mean(x * x, axis=-1, keepdims=True)
    mod = mod_ref[0]
    sh1 = mod[:, 0:D]
    sc1 = mod[:, D:2 * D]
    h = (x * lax.rsqrt(ms + EPS)) * g_ref[...]
    h = h * (1.0 + sc1) + sh1
    hb = h.astype(BF16)

    cosf = cos_ref[...]
    sa = sa_ref[...]
    sb = sb_ref[...]
    bd = bd_ref[...]

    def norm_rope(xs, gain, scale):
        ss = jnp.dot((xs * xs).astype(BF16), bd, preferred_element_type=F32)
        xn = (xs * lax.rsqrt(ss + EPS)) * gain
        out = xn * cosf + pltpu.roll(xn, LANES - 16, 1) * sa + pltpu.roll(xn, 16, 1) * sb
        return out * scale

    hd = ATTN_HEAD_DIM
    qkv = jnp.dot(hb, w_ref[:, _C_Q:_C_QKM], preferred_element_type=F32)
    for j in range(ATTN_WIDTH // LANES):
        qs = norm_rope(qkv[:, j * LANES:(j + 1) * LANES], qg_ref[...], Q_SCALE).astype(BF16)
        q_ref[0, 2 * j] = qs[:, :hd]
        q_ref[0, 2 * j + 1] = qs[:, hd:]
    ks = norm_rope(qkv[:, _C_K:_C_K + LANES], kg_ref[...], 1.0).astype(BF16)
    k_ref[0, 0] = ks[:, :hd]
    k_ref[0, 1] = ks[:, hd:]
    vt = qkv[:, _C_V:_C_V + LANES].T.astype(BF16)
    vt_ref[0, 0] = vt[:hd]
    vt_ref[0, 1] = vt[hd:]
    qkm_ref[0] = jnp.dot(hb, w_ref[:, _C_QKM:_C_VM], preferred_element_type=F32)
    vm_ref[0] = jnp.dot(hb, w_ref[:, _C_VM:_C_OM], preferred_element_type=F32).astype(BF16)
    om_ref[0] = jnp.dot(hb, w_ref[:, _C_OM:_C_G], preferred_element_type=F32)
    gates = jnp.dot(hb, w_ref[:, _C_G:_C_END], preferred_element_type=F32) + gb_ref[...]
    gt = gates.T
    tm = gates.shape[0]
    H = MLSTM_HEADS
    unused = jnp.zeros((GATE_SLOTS - 4, tm), F32)
    for h in range(H):
        grow_ref[0, h] = jnp.concatenate([gt[t * H + h:t * H + h + 1] for t in range(4)] + [unused], axis=0)


def _rope_tables(S):
    rows = S // GRID_W
    row = np.repeat(np.arange(rows, dtype=np.float64), GRID_W)
    col = np.tile(np.arange(GRID_W, dtype=np.float64), rows)
    axis_dim = ATTN_HEAD_DIM // 2
    inv_freq = (ROPE_THETA ** (-np.arange(0, axis_dim, 2, dtype=np.float32) / axis_dim)).astype(np.float32)
    ar = (row[:, None].astype(np.float32) * inv_freq[None, :]).astype(np.float64)
    ac = (col[:, None].astype(np.float32) * inv_freq[None, :]).astype(np.float64)
    cr, sr, cc, sc = np.cos(ar), np.sin(ar), np.cos(ac), np.sin(ac)
    z = np.zeros_like(sr)
    cos_h = np.concatenate([cr, cr, cc, cc], axis=-1)
    sa_h = np.concatenate([-sr, z, -sc, z], axis=-1)
    sb_h = np.concatenate([z, sr, z, sc], axis=-1)
    two = lambda a: jnp.asarray(np.concatenate([a, a], axis=-1), F32)
    return two(cos_h), two(sa_h), two(sb_h)


def _inproj(x, mod3, norm_g, w_in, gate_b, q_g, k_g):
    B, S, D = x.shape
    tm = TM_IN
    H = MLSTM_HEADS
    w = jnp.pad(w_in.astype(BF16), ((0, 0), (0, _C_END - w_in.shape[1])))
    gb = jnp.pad(gate_b.reshape(1, -1), ((0, 0), (0, LANES - N_GATES)))
    cosf, sa, sb = _rope_tables(S)
    qg2 = jnp.concatenate([q_g, q_g]).reshape(1, LANES)
    kg2 = jnp.concatenate([k_g, k_g]).reshape(1, LANES)
    blk = np.kron(np.eye(LANES // ATTN_HEAD_DIM), np.ones((ATTN_HEAD_DIM, ATTN_HEAD_DIM))) / ATTN_HEAD_DIM
    bd = jnp.asarray(blk, BF16)

    const = lambda shape: pl.BlockSpec(shape, lambda b, i: (0,) * len(shape))
    hd = ATTN_HEAD_DIM
    out_shape = (
        jax.ShapeDtypeStruct((B, ATTN_HEADS, S, hd), BF16),
        jax.ShapeDtypeStruct((B, ATTN_KV_HEADS, S, hd), BF16),
        jax.ShapeDtypeStruct((B, ATTN_KV_HEADS, hd, S), BF16),
        jax.ShapeDtypeStruct((B, S, 2 * MLSTM_WIDTH), F32),
        jax.ShapeDtypeStruct((B, S, MLSTM_WIDTH), BF16),
        jax.ShapeDtypeStruct((B, S, MLSTM_WIDTH), F32),
        jax.ShapeDtypeStruct((B, H, GATE_SLOTS, S), F32),
    )
    out_specs = (
        pl.BlockSpec((1, ATTN_HEADS, tm, hd), lambda b, i: (b, 0, i, 0)),
        pl.BlockSpec((1, ATTN_KV_HEADS, tm, hd), lambda b, i: (b, 0, i, 0)),
        pl.BlockSpec((1, ATTN_KV_HEADS, hd, tm), lambda b, i: (b, 0, 0, i)),
        pl.BlockSpec((1, tm, 2 * MLSTM_WIDTH), lambda b, i: (b, i, 0)),
        pl.BlockSpec((1, tm, MLSTM_WIDTH), lambda b, i: (b, i, 0)),
        pl.BlockSpec((1, tm, MLSTM_WIDTH), lambda b, i: (b, i, 0)),
        pl.BlockSpec((1, H, GATE_SLOTS, tm), lambda b, i: (b, 0, 0, i)),
    )
    in_specs = [
        pl.BlockSpec((1, tm, D), lambda b, i: (b, i, 0)),
        pl.BlockSpec((1, 1, 6 * D), lambda b, i: (b, 0, 0)),
        const((1, D)),
        const((D, _C_END)),
        const((1, LANES)),
        const((1, LANES)),
        const((1, LANES)),
        pl.BlockSpec((tm, LANES), lambda b, i: (i, 0)),
        pl.BlockSpec((tm, LANES), lambda b, i: (i, 0)),
        pl.BlockSpec((tm, LANES), lambda b, i: (i, 0)),
        const((LANES, LANES)),
    ]
    return pl.pallas_call(
        _inproj_kernel,
        grid=(B, S // tm),
        in_specs=in_specs,
        out_specs=out_specs,
        out_shape=out_shape,
        compiler_params=pltpu.CompilerParams(
            dimension_semantics=("parallel", "parallel"), vmem_limit_bytes=VMEM_LIMIT),
        name="inproj",
    )(x, mod3, norm_g.reshape(1, D), w, gb, qg2, kg2, cosf, sa, sb, bd)


IN_PROJ_GATE0 = ATTN_WIDTH + 2 * KV_WIDTH + 4 * MLSTM_WIDTH


def _attn_kernel(bound_ref, q_ref, k_ref, vt_ref, o_ref, s0_ref, vx_ref):
    G = ATTN_GROUP
    hd = ATTN_HEAD_DIM
    S = k_ref.shape[2]
    R = G * TQ
    nk = S // TK
    nq = S // TQ

    vx_ref[0:hd] = vt_ref[0, 0]
    extra = lax.broadcasted_iota(jnp.int32, (VX_ROWS - hd, S), 0)
    vx_ref[hd:VX_ROWS] = (extra == 0).astype(BF16)

    def load_q(qi):
        r0 = pl.multiple_of(qi * TQ, TQ)
        return jnp.concatenate([q_ref[0, g, pl.ds(r0, TQ), :] for g in range(G)], axis=0)

    def scores(qb, kj):
        kb = k_ref[0, 0, kj * TK:(kj + 1) * TK, :]
        return _nt_dot(kb, qb)

    def update(kj, s, m, acc):
        mn = jnp.maximum(m, jnp.max(s, axis=0, keepdims=True))
        alpha = jnp.exp2(m - mn)
        p = jnp.exp2(s - mn).astype(BF16)
        vb = vx_ref[:, kj * TK:(kj + 1) * TK]
        acc = alpha * acc + jnp.dot(vb, p, preferred_element_type=F32)
        return mn, acc

    def finish(qi, num, den):
        o = (num / den).astype(BF16)
        r0 = pl.multiple_of(qi * TQ, TQ)
        for g in range(G):
            o_ref[0, g * hd:(g + 1) * hd, pl.ds(r0, TQ)] = o[:, g * TQ:(g + 1) * TQ]

    bound = bound_ref[0]

    @pl.when(bound <= SAFE_SCORE_BOUND)
    def _():
        def qblock(qi, carry):
            qb = load_q(qi)
            acc = jnp.zeros((hd, R), F32)
            l = jnp.zeros((F32_SUBLANES, R), F32)
            ahead = [scores(qb, kj) for kj in range(SCORE_TILES_AHEAD)]
            for kj in range(nk):
                if kj + SCORE_TILES_AHEAD < nk:
                    ahead.append(scores(qb, kj + SCORE_TILES_AHEAD))
                p = jnp.exp2(ahead[kj] - bound)
                l = l + jnp.sum(p.reshape(TK // F32_SUBLANES, F32_SUBLANES, R), axis=0)
                acc = acc + jnp.dot(vt_ref[0, 0, :, kj * TK:(kj + 1) * TK], p.astype(BF16),
                                    preferred_element_type=F32)
            finish(qi, acc, jnp.sum(l, axis=0, keepdims=True))
            return carry

        lax.fori_loop(0, nq, qblock, 0, unroll=2)

    @pl.when(bound > SAFE_SCORE_BOUND)
    def _():
        s0_ref[...] = scores(load_q(0), 0)

        def qblock(qi, carry):
            qb = load_q(qi)
            ss = [scores(qb, kj) for kj in range(1, nk)]
            s_next = scores(load_q(jnp.minimum(qi + 1, nq - 1)), 0)
            m = jnp.full((1, R), -jnp.inf, F32)
            acc = jnp.zeros((VX_ROWS, R), F32)
            m, acc = update(0, s0_ref[...], m, acc)
            for kj in range(1, nk):
                m, acc = update(kj, ss[kj - 1], m, acc)
            s0_ref[...] = s_next
            finish(qi, acc[0:hd], acc[hd:hd + 1])
            return carry

        lax.fori_loop(0, nq, qblock, 0)


def _attention(q, k, vt, q_g, k_g):
    B, _, S, hd = q.shape
    G = ATTN_GROUP
    bound = (SCORE_BOUND_MARGIN * hd * Q_SCALE * jnp.max(jnp.abs(q_g)) * jnp.max(jnp.abs(k_g))).reshape(1)
    return pl.pallas_call(
        _attn_kernel,
        grid=(B, ATTN_KV_HEADS),
        in_specs=[
            pl.BlockSpec(memory_space=pltpu.SMEM),
            pl.BlockSpec((1, G, S, hd), lambda b, kv: (b, kv, 0, 0)),
            pl.BlockSpec((1, 1, S, hd), lambda b, kv: (b, kv, 0, 0)),
            pl.BlockSpec((1, 1, hd, S), lambda b, kv: (b, kv, 0, 0)),
        ],
        out_specs=pl.BlockSpec((1, G * hd, S), lambda b, kv: (b, kv, 0)),
        out_shape=jax.ShapeDtypeStruct((B, ATTN_WIDTH, S), BF16),
        scratch_shapes=[pltpu.VMEM((TK, G * TQ), F32),
                        pltpu.VMEM((VX_ROWS, S), BF16)],
        compiler_params=pltpu.CompilerParams(
            dimension_semantics=("parallel", "parallel"), vmem_limit_bytes=VMEM_LIMIT),
        name="attn",
    )(bound.astype(F32), q, k, vt)


def _log_sigmoid(x):
    return jnp.minimum(x, 0.0) - jnp.log(1.0 + jnp.exp(-jnp.abs(x)))


def _mlstm_kernel(qpre_ref, kpre_ref, v_ref, o_ref, grow_ref, cwq_ref, cwk_ref, cbq_ref, cbk_ref,
                  ng_ref, y_ref, xp_ref, qs_ref, ks_ref, kt_ref, rows_ref, col_ref, hf_ref, hb_ref, ct_ref):
    S = v_ref.shape[1]
    L = CHUNK
    assert L == LANES, "chunk tiles are square (L, 128) tiles"
    nc = S // L
    Dh = MLSTM_HEAD_DIM

    pad = 8
    zeros_pad = jnp.zeros((pad, LANES), F32)
    xp_ref[0:pad] = zeros_pad
    xp_ref[pad + S:pad + S + pad] = zeros_pad
    RB = 512
    for src_ref, cw_ref, cb_ref, dst_ref, scale in (
            (qpre_ref, cwq_ref, cbq_ref, qs_ref, 1.0),
            (kpre_ref, cwk_ref, cbk_ref, ks_ref, Dh ** -0.5)):
        xp_ref[pad:pad + S] = src_ref[0]
        cw = cw_ref[...]
        cb = cb_ref[...]
        for r in range(S // RB):
            acc = jnp.zeros((RB, LANES), F32) + cb
            for j in range(CONV_WIDTH):
                off = pad + r * RB + j - CONV_WIDTH // 2
                acc = acc + xp_ref[off:off + RB] * cw[j:j + 1]
            y = acc * jax.nn.sigmoid(acc) * scale
            dst_ref[r * RB:(r + 1) * RB] = y.astype(BF16)
            if dst_ref is ks_ref:
                kt_ref[:, r * RB:(r + 1) * RB] = y.T.astype(BF16)

    ri = lax.broadcasted_iota(jnp.int32, (L, L), 0)
    ci = lax.broadcasted_iota(jnp.int32, (L, L), 1)
    lower = ci <= ri
    upper = ri <= ci
    tril_b = lower.astype(BF16)
    triu_b = upper.astype(BF16)

    def split3(x):
        t1 = x.astype(BF16).astype(F32)
        e1 = x - t1
        t2 = e1.astype(BF16).astype(F32)
        t3 = (e1 - t2).astype(BF16).astype(F32)
        return t1, t2, t3

    NR = GATE_SLOTS * nc
    graw = jnp.concatenate([grow_ref[0, 0, :, c * L:(c + 1) * L] for c in range(nc)], axis=0)
    tri2 = jnp.concatenate([triu_b, tril_b], axis=1)
    ps = sum(jnp.dot(t.astype(BF16), tri2, preferred_element_type=F32)
             for t in split3(_log_sigmoid(graw) * LOG2E))
    pre, suf = ps[:, :L], ps[:, L:]
    rj = lax.broadcasted_iota(jnp.int32, (NR, L), 0) & (GATE_SLOTS - 1)
    lane = lax.broadcasted_iota(jnp.int32, (NR, L), 1)
    is_fw = rj == 0
    is_r = (rj & 5) == 0
    rr = graw * LOG2E - jnp.where(is_fw, pltpu.roll(pre, NR - 1, 0), pltpu.roll(suf, NR - 1, 0))
    x = jnp.where(is_r, rr, -jnp.inf)
    sh = 1
    while sh < L:
        fwd = jnp.where(lane >= sh, pltpu.roll(x, sh, 1), -jnp.inf)
        bwd = jnp.where(lane < L - sh, pltpu.roll(x, L - sh, 1), -jnp.inf)
        x = jnp.maximum(x, jnp.where(is_fw, fwd, bwd))
        sh *= 2
    quant = jnp.where(is_r, rr, jnp.where(rj == 1, pre, jnp.where(rj == 3, suf, jnp.where(
        (rj & 5) == 4, pltpu.roll(x, 4, 0), 0.0))))
    src_rows = (1, 0, 4, 3, 2, 6)
    ncol = len(src_rows) * LANES
    kk = lax.broadcasted_iota(jnp.int32, (4 * GATE_SLOTS, ncol), 0)
    blk = lax.shift_right_logical(lax.broadcasted_iota(jnp.int32, (4 * GATE_SLOTS, ncol), 1), 7)
    want = jnp.full((4 * GATE_SLOTS, ncol), -1, jnp.int32)
    for q, j in enumerate(src_rows):
        want = jnp.where(blk == q, j, want)
    sel = ((kk & (GATE_SLOTS - 1)) == jnp.where(kk < 3 * GATE_SLOTS, want, -1)).astype(BF16)
    terms = split3(quant)
    zeros8 = jnp.zeros((GATE_SLOTS, L), F32)
    for c in range(nc):
        rows = slice(c * GATE_SLOTS, (c + 1) * GATE_SLOTS)
        rows_ref[:, c * L:(c + 1) * L] = quant[rows]
        stacked = jnp.concatenate([t[rows] for t in terms] + [zeros8], axis=0).astype(BF16)
        col_ref[c * L:(c + 1) * L, :] = _tn_dot(stacked, sel)

    ones_blk = jnp.ones((L, LANES), BF16)
    ct_ref[...] = jnp.zeros_like(ct_ref)

    def direction(c, m, d_idx, mask, g_row, out_ref):
        c0 = pl.multiple_of(c * L, L)
        qc = qs_ref[pl.ds(c0, L), :]
        kc = ks_ref[pl.ds(c0, L), :]
        ktc = kt_ref[:, pl.ds(c0, L)]
        vc = v_ref[0, pl.ds(c0, L), :]
        r_row = rows_ref[2 * d_idx:2 * d_idx + 1, pl.ds(c0, L)]
        b = col_ref[pl.ds(c0, L), (3 * d_idx) * LANES:(3 * d_idx + 1) * LANES]
        r = col_ref[pl.ds(c0, L), (3 * d_idx + 1) * LANES:(3 * d_idx + 2) * LANES]
        cmax = col_ref[pl.ds(c0, L), (3 * d_idx + 2) * LANES:(3 * d_idx + 3) * LANES]
        rmax = cmax[g_row:g_row + 1]
        p = jnp.exp2(jnp.where(mask, r_row - cmax, -jnp.inf))
        s = (_nt_dot(qc, kc) * p).astype(BF16)
        intra = jnp.dot(s, jnp.concatenate([vc, ones_blk], axis=1), preferred_element_type=F32)
        w = jnp.exp2(r - rmax)
        wv = jnp.concatenate([(w * vc.astype(F32)).astype(BF16), w.astype(BF16)], axis=1)
        upd = jnp.dot(ktc, wv, preferred_element_type=F32)
        mc = jnp.maximum(m, cmax)
        w_inter = jnp.exp2(m - mc)
        w_intra = jnp.exp2(cmax - mc)
        ct = ct_ref[d_idx]
        inter = jnp.dot(qc, ct.astype(BF16), preferred_element_type=F32)
        num = w_inter * inter[:, :Dh] + w_intra * intra[:, :Dh]
        den = w_inter * inter[:, Dh:] + w_intra * intra[:, Dh:]
        out_ref[pl.ds(c0, L), :] = num / jnp.maximum(jnp.abs(den), jnp.exp2(-(b + mc)))
        g = b[g_row:g_row + 1]
        mu = jnp.maximum(m, rmax)
        decay = jnp.exp2(m - mu)
        gain = jnp.exp2(rmax - mu)
        ct_ref[d_idx] = jnp.concatenate([decay, decay], axis=1) * ct \
            + jnp.concatenate([gain, gain], axis=1) * upd
        return g + mu

    def step(j, carry):
        m_f, m_b = carry
        m_f = direction(j, m_f, 0, lower, L - 1, hf_ref)
        m_b = direction(nc - 1 - j, m_b, 1, upper, 0, hb_ref)
        return m_f, m_b

    zero_row = jnp.zeros((1, LANES), F32)
    lax.fori_loop(0, nc, step, (zero_row, zero_row), unroll=4)

    ng = ng_ref[...]
    for r in range(S // RB):
        sl = slice(r * RB, (r + 1) * RB)
        hsum = hf_ref[sl] + hb_ref[sl]
        hn = hsum * lax.rsqrt(jnp.mean(hsum * hsum, axis=-1, keepdims=True) + EPS)
        hn = hn * ng
        y_ref[0, sl] = (hn * jax.nn.sigmoid(o_ref[0, sl])).astype(BF16)


def _mlstm(qkm, vm, om, grow, conv_w, conv_b, norm_g):
    B, S, _ = vm.shape
    H = MLSTM_HEADS
    Dh = MLSTM_HEAD_DIM
    seq = lambda off: pl.BlockSpec((1, S, Dh), lambda b, h: (b, 0, h + off))
    return pl.pallas_call(
        _mlstm_kernel,
        grid=(B, H),
        in_specs=[
            seq(0), seq(H), seq(0), seq(0),
            pl.BlockSpec((1, 1, GATE_SLOTS, S), lambda b, h: (b, h, 0, 0)),
            pl.BlockSpec((CONV_WIDTH, Dh), lambda b, h: (0, h)),
            pl.BlockSpec((CONV_WIDTH, Dh), lambda b, h: (0, h + H)),
            pl.BlockSpec((1, Dh), lambda b, h: (0, h)),
            pl.BlockSpec((1, Dh), lambda b, h: (0, h + H)),
            pl.BlockSpec((1, Dh), lambda b, h: (0, h)),
        ],
        out_specs=pl.BlockSpec((1, S, Dh), lambda b, h: (b, 0, h)),
        out_shape=jax.ShapeDtypeStruct((B, S, MLSTM_WIDTH), BF16),
        scratch_shapes=[
            pltpu.VMEM((S + 16, LANES), F32),
            pltpu.VMEM((S, Dh), BF16),
            pltpu.VMEM((S, Dh), BF16),
            pltpu.VMEM((Dh, S), BF16),
            pltpu.VMEM((GATE_SLOTS, S), F32),
            pltpu.VMEM((S, 6 * LANES), F32),
            pltpu.VMEM((S, Dh), F32),
            pltpu.VMEM((S, Dh), F32),
            pltpu.VMEM((2, Dh, 2 * Dh), F32),
        ],
        compiler_params=pltpu.CompilerParams(
            dimension_semantics=("parallel", "parallel"), vmem_limit_bytes=VMEM_LIMIT),
        name="mlstm",
    )(qkm, qkm, vm, om, grow, conv_w, conv_w, conv_b.reshape(1, -1), conv_b.reshape(1, -1),
      norm_g.reshape(1, -1))


def _outffn_kernel(x_ref, mod_ref, yat_ref, ym_ref, wo_ref, gf_ref, wg_ref, wu_ref, wd_ref, fg_ref, o_ref):
    D = D_MODEL
    x = x_ref[0]
    mod = mod_ref[0]
    g1 = mod[:, 2 * D:3 * D]
    sh2 = mod[:, 3 * D:4 * D]
    sc2 = mod[:, 4 * D:5 * D]
    g2 = mod[:, 5 * D:6 * D]
    mix = _tn_dot(yat_ref[0], wo_ref[0:ATTN_WIDTH, :]) \
        + jnp.dot(ym_ref[0], wo_ref[ATTN_WIDTH:MIX_WIDTH, :], preferred_element_type=F32)
    x1 = x + g1 * mix
    ms = jnp.mean(x1 * x1, axis=-1, keepdims=True)
    h = (x1 * lax.rsqrt(ms + EPS)) * gf_ref[...]
    hb = (h * (1.0 + sc2) + sh2).astype(BF16)
    gate = jnp.dot(hb, wg_ref[...], preferred_element_type=F32)
    up = jnp.dot(hb, wu_ref[...], preferred_element_type=F32)
    act = (gate * jax.nn.sigmoid(gate) * up).astype(BF16)
    ffn = jnp.dot(act, wd_ref[...], preferred_element_type=F32)
    x2 = x1 + g2 * ffn
    ms2 = jnp.mean(x2 * x2, axis=-1, keepdims=True)
    o_ref[0] = (x2 * lax.rsqrt(ms2 + EPS)) * fg_ref[...]


def _outffn(x, mod3, yat, ym, w_out, norm_ffn_g, w_gate, w_up, w_down, final_g):
    B, S, D = x.shape
    tm = TM_FFN
    F = w_gate.shape[1]
    const = lambda shape: pl.BlockSpec(shape, lambda b, i: (0,) * len(shape), pipeline_mode=pl.Buffered(1))
    return pl.pallas_call(
        _outffn_kernel,
        grid=(B, S // tm),
        in_specs=[
            pl.BlockSpec((1, tm, D), lambda b, i: (b, i, 0)),
            pl.BlockSpec((1, 1, 6 * D), lambda b, i: (b, 0, 0)),
            pl.BlockSpec((1, ATTN_WIDTH, tm), lambda b, i: (b, 0, i)),
            pl.BlockSpec((1, tm, MLSTM_WIDTH), lambda b, i: (b, i, 0)),
            const((MIX_WIDTH, D)),
            const((1, D)),
            const((D, F)),
            const((D, F)),
            const((F, D)),
            const((1, D)),
        ],
        out_specs=pl.BlockSpec((1, tm, D), lambda b, i: (b, i, 0)),
        out_shape=jax.ShapeDtypeStruct((B, S, D), F32),
        compiler_params=pltpu.CompilerParams(
            dimension_semantics=("parallel", "parallel"), vmem_limit_bytes=VMEM_LIMIT),
        name="outffn",
    )(x, mod3, yat, ym, w_out.astype(BF16), norm_ffn_g.reshape(1, D), w_gate.astype(BF16),
      w_up.astype(BF16), w_down.astype(BF16), final_g.reshape(1, D))


def kernel(x, c, w_ada, b_ada, norm_mix_g, w_in, q_norm_g, k_norm_g, conv_w, conv_b, gate_b,
           mlstm_norm_g, w_out, norm_ffn_g, w_gate, w_up, w_down, final_norm_g):
    assert w_ada.shape[0] == 1, "the final RMSNorm is fused into the single layer's last kernel"
    B = x.shape[0]
    l = 0
    mod3 = _ada(c, w_ada[l], b_ada[l]).reshape(B, 1, -1)
    q, k, vt, qkm, vm, om, grow = _inproj(x, mod3, norm_mix_g[l], w_in[l], gate_b[l],
                                                q_norm_g[l], k_norm_g[l])
    yat = _attention(q, k, vt, q_norm_g[l], k_norm_g[l])
    ym = _mlstm(qkm, vm, om, grow, conv_w[l], conv_b[l], mlstm_norm_g[l])
    return _outffn(x, mod3, yat, ym, w_out[l], norm_ffn_g[l], w_gate[l], w_up[l], w_down[l], final_norm_g)
```

```python
import functools

import jax
import jax.numpy as jnp
import numpy as np
from jax import lax
from jax.experimental import pallas as pl
from jax.experimental.pallas import tpu as pltpu

F32 = jnp.float32
BF16 = jnp.bfloat16

D_MODEL = 1024
GRID_W = 64
EPS = 1e-6
ATTN_HEADS = 8
ATTN_KV_HEADS = 2
ATTN_GROUP = ATTN_HEADS // ATTN_KV_HEADS
ATTN_HEAD_DIM = 64
ATTN_WIDTH = ATTN_HEADS * ATTN_HEAD_DIM
KV_WIDTH = ATTN_KV_HEADS * ATTN_HEAD_DIM
ROPE_THETA = 10000.0
MLSTM_HEADS = 4
MLSTM_HEAD_DIM = 128
MLSTM_WIDTH = MLSTM_HEADS * MLSTM_HEAD_DIM
CONV_WIDTH = 5
N_GATES = 4 * MLSTM_HEADS
MIX_WIDTH = ATTN_WIDTH + MLSTM_WIDTH
FFN_HIDDEN = 256 * ((8 * D_MODEL + 3 * 256 - 1) // (3 * 256))

LANES = 128
GATE_SLOTS = 8

TM_IN = 512
TM_FFN = 512
TQ = 128
TK = 256
F32_SUBLANES = 8
BF16_SUBLANES = 16
LOG2E = float(np.log2(np.e))
VX_ROWS = ATTN_HEAD_DIM + BF16_SUBLANES
CHUNK = 128
Q_SCALE = ATTN_HEAD_DIM ** -0.5 * float(np.log2(np.e))
VMEM_LIMIT = 56 * 1024 * 1024
SAFE_SCORE_BOUND = 50.0
SCORE_TILES_AHEAD = 3
SCORE_BOUND_MARGIN = 1.02


def _nt_dot(a, b):
    return lax.dot_general(a, b, (((1,), (1,)), ((), ())), preferred_element_type=F32)


def _tn_dot(a, b):
    return lax.dot_general(a, b, (((0,), (0,)), ((), ())), preferred_element_type=F32)


def _ada_kernel(c_ref, w_ref, b_ref, o_ref):
    c = c_ref[...]
    cond = c * jax.nn.sigmoid(c)
    o_ref[...] = jnp.dot(cond, w_ref[...], preferred_element_type=F32,
                         precision=lax.Precision.HIGHEST) + b_ref[...]


def _ada(c, w_ada, b_ada):
    B, D = c.shape
    n = w_ada.shape[1] // D
    return pl.pallas_call(
        _ada_kernel,
        grid=(n,),
        in_specs=[pl.BlockSpec((B, D), lambda j: (0, 0)),
                  pl.BlockSpec((D, D), lambda j: (0, j)),
                  pl.BlockSpec((1, D), lambda j: (0, j))],
        out_specs=pl.BlockSpec((B, D), lambda j: (0, j)),
        out_shape=jax.ShapeDtypeStruct((B, n * D), F32),
        name="ada",
    )(c, w_ada, b_ada.reshape(1, -1))


_C_Q = 0
_C_K = _C_Q + ATTN_WIDTH
_C_V = _C_K + KV_WIDTH
_C_QKM = _C_V + KV_WIDTH
_C_VM = _C_QKM + 2 * MLSTM_WIDTH
_C_OM = _C_VM + MLSTM_WIDTH
_C_G = _C_OM + MLSTM_WIDTH
_C_END = _C_G + LANES


def _inproj_kernel(x_ref, xp_ref, xn_ref, mod_ref, g_ref, w_ref, gb_ref, qg_ref, kg_ref, cos_ref, sa_ref, sb_ref,
                   bd_ref, cw_ref, cb_ref,
                   q_ref, k_ref, vt_ref, qm_ref, ktm_ref, vm_ref, om_ref, grow_ref, cpad_ref):
    D = D_MODEL
    mod = mod_ref[0]
    sh1 = mod[:, 0:D]
    sc1 = mod[:, D:2 * D]

    def normed(x):
        ms = jnp.mean(x * x, axis=-1, keepdims=True)
        h = (x * lax.rsqrt(ms + EPS)) * g_ref[...]
        return (h * (1.0 + sc1) + sh1).astype(BF16)

    hb = normed(x_ref[0])
    hb_halo = normed(jnp.concatenate([xp_ref[0], xn_ref[0]], axis=0))

    cosf = cos_ref[...]
    sa = sa_ref[...]
    sb = sb_ref[...]
    bd = bd_ref[...]

    def norm_rope(xs, gain, scale):
        ss = jnp.dot((xs * xs).astype(BF16), bd, preferred_element_type=F32)
        xn = (xs * lax.rsqrt(ss + EPS)) * gain
        out = xn * cosf + pltpu.roll(xn, LANES - 16, 1) * sa + pltpu.roll(xn, 16, 1) * sb
        return out * scale

    hd = ATTN_HEAD_DIM
    qkv = jnp.dot(hb, w_ref[:, _C_Q:_C_QKM], preferred_element_type=F32)
    for j in range(ATTN_WIDTH // LANES):
        qs = norm_rope(qkv[:, j * LANES:(j + 1) * LANES], qg_ref[...], Q_SCALE).astype(BF16)
        q_ref[0, 2 * j] = qs[:, :hd]
        q_ref[0, 2 * j + 1] = qs[:, hd:]
    ks = norm_rope(qkv[:, _C_K:_C_K + LANES], kg_ref[...], 1.0).astype(BF16)
    k_ref[0, 0] = ks[:, :hd]
    k_ref[0, 1] = ks[:, hd:]
    vt = qkv[:, _C_V:_C_V + LANES].T.astype(BF16)
    vt_ref[0, 0] = vt[:hd]
    vt_ref[0, 1] = vt[hd:]
    tm = hb.shape[0]
    halo = F32_SUBLANES
    qk_pre = jnp.dot(jnp.concatenate([hb, hb_halo], axis=0), w_ref[:, _C_QKM:_C_VM],
                     preferred_element_type=F32)
    i = pl.program_id(1)
    before = jnp.where(i > 0, qk_pre[tm:tm + halo], 0.0)
    after = jnp.where(i < pl.num_programs(1) - 1, qk_pre[tm + halo:], 0.0)
    rb = LANES
    for j0 in range(0, 2 * MLSTM_WIDTH, LANES):
        cols = slice(j0, j0 + LANES)
        slab = j0 // LANES
        cpad_ref[slab, 0:halo] = before[:, cols]
        cpad_ref[slab, halo:halo + tm] = qk_pre[0:tm, cols]
        cpad_ref[slab, halo + tm:2 * halo + tm] = after[:, cols]
        for r0 in range(0, tm, rb):
            acc = jnp.zeros((rb, LANES), F32) + cb_ref[:, cols]
            for t in range(CONV_WIDTH):
                off = halo + r0 + t - CONV_WIDTH // 2
                acc = acc + cpad_ref[slab, off:off + rb, :] * cw_ref[t:t + 1, cols]
            y = acc * jax.nn.sigmoid(acc)
            if j0 < MLSTM_WIDTH:
                qm_ref[0, r0:r0 + rb, cols] = y.astype(BF16)
            else:
                ktm_ref[0, j0 - MLSTM_WIDTH:j0 - MLSTM_WIDTH + LANES, r0:r0 + rb] = \
                    (y * MLSTM_HEAD_DIM ** -0.5).T.astype(BF16)
    vm_ref[0] = jnp.dot(hb, w_ref[:, _C_VM:_C_OM], preferred_element_type=F32).astype(BF16)
    om_ref[0] = jnp.dot(hb, w_ref[:, _C_OM:_C_G], preferred_element_type=F32)
    gates = jnp.dot(hb, w_ref[:, _C_G:_C_END], preferred_element_type=F32) + gb_ref[...]
    gt = gates.T
    tm = gates.shape[0]
    H = MLSTM_HEADS
    unused = jnp.zeros((GATE_SLOTS - 4, tm), F32)
    for h in range(H):
        grow_ref[0, h] = jnp.concatenate([gt[t * H + h:t * H + h + 1] for t in range(4)] + [unused], axis=0)


def _rope_tables(S):
    rows = S // GRID_W
    row = np.repeat(np.arange(rows, dtype=np.float64), GRID_W)
    col = np.tile(np.arange(GRID_W, dtype=np.float64), rows)
    axis_dim = ATTN_HEAD_DIM // 2
    inv_freq = (ROPE_THETA ** (-np.arange(0, axis_dim, 2, dtype=np.float32) / axis_dim)).astype(np.float32)
    ar = (row[:, None].astype(np.float32) * inv_freq[None, :]).astype(np.float64)
    ac = (col[:, None].astype(np.float32) * inv_freq[None, :]).astype(np.float64)
    cr, sr, cc, sc = np.cos(ar), np.sin(ar), np.cos(ac), np.sin(ac)
    z = np.zeros_like(sr)
    cos_h = np.concatenate([cr, cr, cc, cc], axis=-1)
    sa_h = np.concatenate([-sr, z, -sc, z], axis=-1)
    sb_h = np.concatenate([z, sr, z, sc], axis=-1)
    two = lambda a: jnp.asarray(np.concatenate([a, a], axis=-1), F32)
    return two(cos_h), two(sa_h), two(sb_h)


def _inproj(x, mod3, norm_g, w_in, gate_b, q_g, k_g, conv_w, conv_b):
    B, S, D = x.shape
    tm = TM_IN
    H = MLSTM_HEADS
    w = jnp.pad(w_in.astype(BF16), ((0, 0), (0, _C_END - w_in.shape[1])))
    gb = jnp.pad(gate_b.reshape(1, -1), ((0, 0), (0, LANES - N_GATES)))
    cosf, sa, sb = _rope_tables(S)
    qg2 = jnp.concatenate([q_g, q_g]).reshape(1, LANES)
    kg2 = jnp.concatenate([k_g, k_g]).reshape(1, LANES)
    blk = np.kron(np.eye(LANES // ATTN_HEAD_DIM), np.ones((ATTN_HEAD_DIM, ATTN_HEAD_DIM))) / ATTN_HEAD_DIM
    bd = jnp.asarray(blk, BF16)

    const = lambda shape: pl.BlockSpec(shape, lambda b, i: (0,) * len(shape))
    hd = ATTN_HEAD_DIM
    out_shape = (
        jax.ShapeDtypeStruct((B, ATTN_HEADS, S, hd), BF16),
        jax.ShapeDtypeStruct((B, ATTN_KV_HEADS, S, hd), BF16),
        jax.ShapeDtypeStruct((B, ATTN_KV_HEADS, hd, S), BF16),
        jax.ShapeDtypeStruct((B, S, MLSTM_WIDTH), BF16),
        jax.ShapeDtypeStruct((B, MLSTM_WIDTH, S), BF16),
        jax.ShapeDtypeStruct((B, S, MLSTM_WIDTH), BF16),
        jax.ShapeDtypeStruct((B, S, MLSTM_WIDTH), F32),
        jax.ShapeDtypeStruct((B, H, GATE_SLOTS, S), F32),
    )
    out_specs = (
        pl.BlockSpec((1, ATTN_HEADS, tm, hd), lambda b, i: (b, 0, i, 0)),
        pl.BlockSpec((1, ATTN_KV_HEADS, tm, hd), lambda b, i: (b, 0, i, 0)),
        pl.BlockSpec((1, ATTN_KV_HEADS, hd, tm), lambda b, i: (b, 0, 0, i)),
        pl.BlockSpec((1, tm, MLSTM_WIDTH), lambda b, i: (b, i, 0)),
        pl.BlockSpec((1, MLSTM_WIDTH, tm), lambda b, i: (b, 0, i)),
        pl.BlockSpec((1, tm, MLSTM_WIDTH), lambda b, i: (b, i, 0)),
        pl.BlockSpec((1, tm, MLSTM_WIDTH), lambda b, i: (b, i, 0)),
        pl.BlockSpec((1, H, GATE_SLOTS, tm), lambda b, i: (b, 0, 0, i)),
    )
    halo = F32_SUBLANES
    tiles = tm // halo
    in_specs = [
        pl.BlockSpec((1, tm, D), lambda b, i: (b, i, 0)),
        pl.BlockSpec((1, halo, D), lambda b, i: (b, jnp.maximum(i * tiles - 1, 0), 0)),
        pl.BlockSpec((1, halo, D), lambda b, i: (b, jnp.minimum((i + 1) * tiles, S // halo - 1), 0)),
        pl.BlockSpec((1, 1, 6 * D), lambda b, i: (b, 0, 0)),
        const((1, D)),
        const((D, _C_END)),
        const((1, LANES)),
        const((1, LANES)),
        const((1, LANES)),
        pl.BlockSpec((tm, LANES), lambda b, i: (i, 0)),
        pl.BlockSpec((tm, LANES), lambda b, i: (i, 0)),
        pl.BlockSpec((tm, LANES), lambda b, i: (i, 0)),
        const((LANES, LANES)),
        const((CONV_WIDTH, 2 * MLSTM_WIDTH)),
        const((1, 2 * MLSTM_WIDTH)),
    ]
    return pl.pallas_call(
        _inproj_kernel,
        grid=(B, S // tm),
        in_specs=in_specs,
        out_specs=out_specs,
        out_shape=out_shape,
        scratch_shapes=[
            pltpu.VMEM((2 * MLSTM_WIDTH // LANES, tm + 2 * halo, LANES), F32)],
        compiler_params=pltpu.CompilerParams(
            dimension_semantics=("parallel", "parallel"), vmem_limit_bytes=VMEM_LIMIT),
        name="inproj",
    )(x, x, x, mod3, norm_g.reshape(1, D), w, gb, qg2, kg2, cosf, sa, sb, bd, conv_w, conv_b.reshape(1, -1))


def _attn_kernel(bound_ref, q_ref, k_ref, vt_ref, o_ref, s0_ref, vx_ref):
    G = ATTN_GROUP
    hd = ATTN_HEAD_DIM
    S = k_ref.shape[2]
    R = G * TQ
    nk = S // TK
    nq = S // TQ

    vx_ref[0:hd] = vt_ref[0, 0]
    extra = lax.broadcasted_iota(jnp.int32, (VX_ROWS - hd, S), 0)
    vx_ref[hd:VX_ROWS] = (extra == 0).astype(BF16)

    def load_q(qi):
        r0 = pl.multiple_of(qi * TQ, TQ)
        return jnp.concatenate([q_ref[0, g, pl.ds(r0, TQ), :] for g in range(G)], axis=0)

    def scores(qb, kj):
        kb = k_ref[0, 0, kj * TK:(kj + 1) * TK, :]
        return _nt_dot(kb, qb)

    def update(kj, s, m, acc):
        mn = jnp.maximum(m, jnp.max(s, axis=0, keepdims=True))
        alpha = jnp.exp2(m - mn)
        p = jnp.exp2(s - mn).astype(BF16)
        vb = vx_ref[:, kj * TK:(kj + 1) * TK]
        acc = alpha * acc + jnp.dot(vb, p, preferred_element_type=F32)
        return mn, acc

    def finish(qi, num, den):
        o = (num / den).astype(BF16)
        r0 = pl.multiple_of(qi * TQ, TQ)
        for g in range(G):
            o_ref[0, g * hd:(g + 1) * hd, pl.ds(r0, TQ)] = o[:, g * TQ:(g + 1) * TQ]

    bound = bound_ref[0]

    @pl.when(bound <= SAFE_SCORE_BOUND)
    def _():
        def qblock(qi, carry):
            qb = load_q(qi)
            acc = jnp.zeros((hd, R), F32)
            l = jnp.zeros((F32_SUBLANES, R), F32)
            ahead = [scores(qb, kj) for kj in range(SCORE_TILES_AHEAD)]
            for kj in range(nk):
                if kj + SCORE_TILES_AHEAD < nk:
                    ahead.append(scores(qb, kj + SCORE_TILES_AHEAD))
                p = jnp.exp2(ahead[kj] - bound)
                l = l + jnp.sum(p.reshape(TK // F32_SUBLANES, F32_SUBLANES, R), axis=0)
                acc = acc + jnp.dot(vt_ref[0, 0, :, kj * TK:(kj + 1) * TK], p.astype(BF16),
                                    preferred_element_type=F32)
            finish(qi, acc, jnp.sum(l, axis=0, keepdims=True))
            return carry

        lax.fori_loop(0, nq, qblock, 0, unroll=2)

    @pl.when(bound > SAFE_SCORE_BOUND)
    def _():
        s0_ref[...] = scores(load_q(0), 0)

        def qblock(qi, carry):
            qb = load_q(qi)
            ss = [scores(qb, kj) for kj in range(1, nk)]
            s_next = scores(load_q(jnp.minimum(qi + 1, nq - 1)), 0)
            m = jnp.full((1, R), -jnp.inf, F32)
            acc = jnp.zeros((VX_ROWS, R), F32)
            m, acc = update(0, s0_ref[...], m, acc)
            for kj in range(1, nk):
                m, acc = update(kj, ss[kj - 1], m, acc)
            s0_ref[...] = s_next
            finish(qi, acc[0:hd], acc[hd:hd + 1])
            return carry

        lax.fori_loop(0, nq, qblock, 0)


def _attention(q, k, vt, q_g, k_g):
    B, _, S, hd = q.shape
    G = ATTN_GROUP
    bound = (SCORE_BOUND_MARGIN * hd * Q_SCALE * jnp.max(jnp.abs(q_g)) * jnp.max(jnp.abs(k_g))).reshape(1)
    return pl.pallas_call(
        _attn_kernel,
        grid=(B, ATTN_KV_HEADS),
        in_specs=[
            pl.BlockSpec(memory_space=pltpu.SMEM),
            pl.BlockSpec((1, G, S, hd), lambda b, kv: (b, kv, 0, 0)),
            pl.BlockSpec((1, 1, S, hd), lambda b, kv: (b, kv, 0, 0)),
            pl.BlockSpec((1, 1, hd, S), lambda b, kv: (b, kv, 0, 0)),
        ],
        out_specs=pl.BlockSpec((1, G * hd, S), lambda b, kv: (b, kv, 0)),
        out_shape=jax.ShapeDtypeStruct((B, ATTN_WIDTH, S), BF16),
        scratch_shapes=[pltpu.VMEM((TK, G * TQ), F32),
                        pltpu.VMEM((VX_ROWS, S), BF16)],
        compiler_params=pltpu.CompilerParams(
            dimension_semantics=("parallel", "parallel"), vmem_limit_bytes=VMEM_LIMIT),
        name="attn",
    )(bound.astype(F32), q, k, vt)


def _log_sigmoid(x):
    return jnp.minimum(x, 0.0) - jnp.log(1.0 + jnp.exp(-jnp.abs(x)))


def _mlstm_kernel(qs_ref, kt_ref, v_ref, o_ref, grow_ref, ng_ref, y_ref,
                  rows_ref, col_ref, hf_ref, hb_ref, ct_ref):
    S = v_ref.shape[1]
    L = CHUNK
    assert L == LANES, "chunk tiles are square (L, 128) tiles"
    nc = S // L
    Dh = MLSTM_HEAD_DIM
    RB = 512

    ri = lax.broadcasted_iota(jnp.int32, (L, L), 0)
    ci = lax.broadcasted_iota(jnp.int32, (L, L), 1)
    lower = ci <= ri
    upper = ri <= ci
    tril_b = lower.astype(BF16)
    triu_b = upper.astype(BF16)

    def split3(x):
        t1 = x.astype(BF16).astype(F32)
        e1 = x - t1
        t2 = e1.astype(BF16).astype(F32)
        t3 = (e1 - t2).astype(BF16).astype(F32)
        return t1, t2, t3

    NR = GATE_SLOTS * nc
    graw = jnp.concatenate([grow_ref[0, 0, :, c * L:(c + 1) * L] for c in range(nc)], axis=0)
    tri2 = jnp.concatenate([triu_b, tril_b], axis=1)
    ps = sum(jnp.dot(t.astype(BF16), tri2, preferred_element_type=F32)
             for t in split3(_log_sigmoid(graw) * LOG2E))
    pre, suf = ps[:, :L], ps[:, L:]
    rj = lax.broadcasted_iota(jnp.int32, (NR, L), 0) & (GATE_SLOTS - 1)
    lane = lax.broadcasted_iota(jnp.int32, (NR, L), 1)
    is_fw = rj == 0
    is_r = (rj & 5) == 0
    rr = graw * LOG2E - jnp.where(is_fw, pltpu.roll(pre, NR - 1, 0), pltpu.roll(suf, NR - 1, 0))
    x = jnp.where(is_r, rr, -jnp.inf)
    sh = 1
    while sh < L:
        fwd = jnp.where(lane >= sh, pltpu.roll(x, sh, 1), -jnp.inf)
        bwd = jnp.where(lane < L - sh, pltpu.roll(x, L - sh, 1), -jnp.inf)
        x = jnp.maximum(x, jnp.where(is_fw, fwd, bwd))
        sh *= 2
    quant = jnp.where(is_r, rr, jnp.where(rj == 1, pre, jnp.where(rj == 3, suf, jnp.where(
        (rj & 5) == 4, pltpu.roll(x, 4, 0), 0.0))))
    src_rows = (1, 0, 4, 3, 2, 6)
    ncol = len(src_rows) * LANES
    kk = lax.broadcasted_iota(jnp.int32, (4 * GATE_SLOTS, ncol), 0)
    blk = lax.shift_right_logical(lax.broadcasted_iota(jnp.int32, (4 * GATE_SLOTS, ncol), 1), 7)
    want = jnp.full((4 * GATE_SLOTS, ncol), -1, jnp.int32)
    for q, j in enumerate(src_rows):
        want = jnp.where(blk == q, j, want)
    sel = ((kk & (GATE_SLOTS - 1)) == jnp.where(kk < 3 * GATE_SLOTS, want, -1)).astype(BF16)
    terms = split3(quant)
    zeros8 = jnp.zeros((GATE_SLOTS, L), F32)
    for c in range(nc):
        rows = slice(c * GATE_SLOTS, (c + 1) * GATE_SLOTS)
        rows_ref[:, c * L:(c + 1) * L] = quant[rows]
        stacked = jnp.concatenate([t[rows] for t in terms] + [zeros8], axis=0).astype(BF16)
        col_ref[c * L:(c + 1) * L, :] = _tn_dot(stacked, sel)

    ones_blk = jnp.ones((L, LANES), BF16)
    ct_ref[...] = jnp.zeros_like(ct_ref)

    def direction(c, m, d_idx, mask, g_row, out_ref):
        c0 = pl.multiple_of(c * L, L)
        qc = qs_ref[0, pl.ds(c0, L), :]
        ktc = kt_ref[0, :, pl.ds(c0, L)]
        vc = v_ref[0, pl.ds(c0, L), :]
        r_row = rows_ref[2 * d_idx:2 * d_idx + 1, pl.ds(c0, L)]
        b = col_ref[pl.ds(c0, L), (3 * d_idx) * LANES:(3 * d_idx + 1) * LANES]
        r = col_ref[pl.ds(c0, L), (3 * d_idx + 1) * LANES:(3 * d_idx + 2) * LANES]
        cmax = col_ref[pl.ds(c0, L), (3 * d_idx + 2) * LANES:(3 * d_idx + 3) * LANES]
        rmax = cmax[g_row:g_row + 1]
        p = jnp.exp2(jnp.where(mask, r_row - cmax, -jnp.inf))
        s = (jnp.dot(qc, ktc, preferred_element_type=F32) * p).astype(BF16)
        intra = jnp.dot(s, jnp.concatenate([vc, ones_blk], axis=1), preferred_element_type=F32)
        w = jnp.exp2(r - rmax)
        wv = jnp.concatenate([(w * vc.astype(F32)).astype(BF16), w.astype(BF16)], axis=1)
        upd = jnp.dot(ktc, wv, preferred_element_type=F32)
        mc = jnp.maximum(m, cmax)
        w_inter = jnp.exp2(m - mc)
        w_intra = jnp.exp2(cmax - mc)
        ct = ct_ref[d_idx]
        inter = jnp.dot(qc, ct.astype(BF16), preferred_element_type=F32)
        num = w_inter * inter[:, :Dh] + w_intra * intra[:, :Dh]
        den = w_inter * inter[:, Dh:] + w_intra * intra[:, Dh:]
        out_ref[pl.ds(c0, L), :] = num / jnp.maximum(jnp.abs(den), jnp.exp2(-(b + mc)))
        g = b[g_row:g_row + 1]
        mu = jnp.maximum(m, rmax)
        decay = jnp.exp2(m - mu)
        gain = jnp.exp2(rmax - mu)
        ct_ref[d_idx] = jnp.concatenate([decay, decay], axis=1) * ct \
            + jnp.concatenate([gain, gain], axis=1) * upd
        return g + mu

    def step(j, carry):
        m_f, m_b = carry
        m_f = direction(j, m_f, 0, lower, L - 1, hf_ref)
        m_b = direction(nc - 1 - j, m_b, 1, upper, 0, hb_ref)
        return m_f, m_b

    zero_row = jnp.zeros((1, LANES), F32)
    lax.fori_loop(0, nc, step, (zero_row, zero_row), unroll=4)

    ng = ng_ref[...]
    for r in range(S // RB):
        sl = slice(r * RB, (r + 1) * RB)
        hsum = hf_ref[sl] + hb_ref[sl]
        hn = hsum * lax.rsqrt(jnp.mean(hsum * hsum, axis=-1, keepdims=True) + EPS)
        hn = hn * ng
        y_ref[0, sl] = (hn * jax.nn.sigmoid(o_ref[0, sl])).astype(BF16)


def _mlstm(qm, ktm, vm, om, grow, norm_g):
    B, S, _ = vm.shape
    H = MLSTM_HEADS
    Dh = MLSTM_HEAD_DIM
    seq = pl.BlockSpec((1, S, Dh), lambda b, h: (b, 0, h))
    return pl.pallas_call(
        _mlstm_kernel,
        grid=(B, H),
        in_specs=[
            seq,
            pl.BlockSpec((1, Dh, S), lambda b, h: (b, h, 0)),
            seq, seq,
            pl.BlockSpec((1, 1, GATE_SLOTS, S), lambda b, h: (b, h, 0, 0)),
            pl.BlockSpec((1, Dh), lambda b, h: (0, h)),
        ],
        out_specs=pl.BlockSpec((1, S, Dh), lambda b, h: (b, 0, h)),
        out_shape=jax.ShapeDtypeStruct((B, S, MLSTM_WIDTH), BF16),
        scratch_shapes=[
            pltpu.VMEM((GATE_SLOTS, S), F32),
            pltpu.VMEM((S, 6 * LANES), F32),
            pltpu.VMEM((S, Dh), F32),
            pltpu.VMEM((S, Dh), F32),
            pltpu.VMEM((2, Dh, 2 * Dh), F32),
        ],
        compiler_params=pltpu.CompilerParams(
            dimension_semantics=("parallel", "parallel"), vmem_limit_bytes=VMEM_LIMIT),
        name="mlstm",
    )(qm, ktm, vm, om, grow, norm_g.reshape(1, -1))


def _outffn_kernel(x_ref, mod_ref, yat_ref, ym_ref, wo_ref, gf_ref, wg_ref, wu_ref, wd_ref, fg_ref, o_ref):
    D = D_MODEL
    x = x_ref[0]
    mod = mod_ref[0]
    g1 = mod[:, 2 * D:3 * D]
    sh2 = mod[:, 3 * D:4 * D]
    sc2 = mod[:, 4 * D:5 * D]
    g2 = mod[:, 5 * D:6 * D]
    mix = _tn_dot(yat_ref[0], wo_ref[0:ATTN_WIDTH, :]) \
        + jnp.dot(ym_ref[0], wo_ref[ATTN_WIDTH:MIX_WIDTH, :], preferred_element_type=F32)
    x1 = x + g1 * mix
    ms = jnp.mean(x1 * x1, axis=-1, keepdims=True)
    h = (x1 * lax.rsqrt(ms + EPS)) * gf_ref[...]
    hb = (h * (1.0 + sc2) + sh2).astype(BF16)
    gate = jnp.dot(hb, wg_ref[...], preferred_element_type=F32)
    up = jnp.dot(hb, wu_ref[...], preferred_element_type=F32)
    act = (gate * jax.nn.sigmoid(gate) * up).astype(BF16)
    ffn = jnp.dot(act, wd_ref[...], preferred_element_type=F32)
    x2 = x1 + g2 * ffn
    ms2 = jnp.mean(x2 * x2, axis=-1, keepdims=True)
    o_ref[0] = (x2 * lax.rsqrt(ms2 + EPS)) * fg_ref[...]


def _outffn(x, mod3, yat, ym, w_out, norm_ffn_g, w_gate, w_up, w_down, final_g):
    B, S, D = x.shape
    tm = TM_FFN
    F = w_gate.shape[1]
    const = lambda shape: pl.BlockSpec(shape, lambda b, i: (0,) * len(shape), pipeline_mode=pl.Buffered(1))
    return pl.pallas_call(
        _outffn_kernel,
        grid=(B, S // tm),
        in_specs=[
            pl.BlockSpec((1, tm, D), lambda b, i: (b, i, 0)),
            pl.BlockSpec((1, 1, 6 * D), lambda b, i: (b, 0, 0)),
            pl.BlockSpec((1, ATTN_WIDTH, tm), lambda b, i: (b, 0, i)),
            pl.BlockSpec((1, tm, MLSTM_WIDTH), lambda b, i: (b, i, 0)),
            const((MIX_WIDTH, D)),
            const((1, D)),
            const((D, F)),
            const((D, F)),
            const((F, D)),
            const((1, D)),
        ],
        out_specs=pl.BlockSpec((1, tm, D), lambda b, i: (b, i, 0)),
        out_shape=jax.ShapeDtypeStruct((B, S, D), F32),
        compiler_params=pltpu.CompilerParams(
            dimension_semantics=("parallel", "parallel"), vmem_limit_bytes=VMEM_LIMIT),
        name="outffn",
    )(x, mod3, yat, ym, w_out.astype(BF16), norm_ffn_g.reshape(1, D), w_gate.astype(BF16),
      w_up.astype(BF16), w_down.astype(BF16), final_g.reshape(1, D))


def kernel(x, c, w_ada, b_ada, norm_mix_g, w_in, q_norm_g, k_norm_g, conv_w, conv_b, gate_b,
           mlstm_norm_g, w_out, norm_ffn_g, w_gate, w_up, w_down, final_norm_g):
    assert w_ada.shape[0] == 1, "the final RMSNorm is fused into the single layer's last kernel"
    B = x.shape[0]
    l = 0
    mod3 = _ada(c, w_ada[l], b_ada[l]).reshape(B, 1, -1)
    q, k, vt, qm, ktm, vm, om, grow = _inproj(x, mod3, norm_mix_g[l], w_in[l], gate_b[l],
                                              q_norm_g[l], k_norm_g[l], conv_w[l], conv_b[l])
    yat = _attention(q, k, vt, q_norm_g[l], k_norm_g[l])
    ym = _mlstm(qm, ktm, vm, om, grow, mlstm_norm_g[l])
    return _outffn(x, mod3, yat, ym, w_out[l], norm_ffn_g[l], w_gate[l], w_up[l], w_down[l], final_norm_g)
```

```python
import functools

import jax
import jax.numpy as jnp
import numpy as np
from jax import lax
from jax.experimental import pallas as pl
from jax.experimental.pallas import tpu as pltpu

F32 = jnp.float32
BF16 = jnp.bfloat16

D_MODEL = 1024
GRID_W = 64
EPS = 1e-6
ATTN_HEADS = 8
ATTN_KV_HEADS = 2
ATTN_GROUP = ATTN_HEADS // ATTN_KV_HEADS
ATTN_HEAD_DIM = 64
ATTN_WIDTH = ATTN_HEADS * ATTN_HEAD_DIM
KV_WIDTH = ATTN_KV_HEADS * ATTN_HEAD_DIM
ROPE_THETA = 10000.0
MLSTM_HEADS = 4
MLSTM_HEAD_DIM = 128
MLSTM_WIDTH = MLSTM_HEADS * MLSTM_HEAD_DIM
CONV_WIDTH = 5
N_GATES = 4 * MLSTM_HEADS
MIX_WIDTH = ATTN_WIDTH + MLSTM_WIDTH
FFN_HIDDEN = 256 * ((8 * D_MODEL + 3 * 256 - 1) // (3 * 256))

LANES = 128
GATE_SLOTS = 8

TM_IN = 512
TM_FFN = 512
TQ = 128
TK = 256
F32_SUBLANES = 8
BF16_SUBLANES = 16
LOG2E = float(np.log2(np.e))
VX_ROWS = ATTN_HEAD_DIM + BF16_SUBLANES
CHUNK = 128
Q_SCALE = ATTN_HEAD_DIM ** -0.5 * float(np.log2(np.e))
VMEM_LIMIT = 56 * 1024 * 1024
SAFE_SCORE_BOUND = 50.0
TK_BOUNDED = 256
SCORE_TILES_AHEAD = 3
SCORE_BOUND_MARGIN = 1.02


def _nt_dot(a, b):
    return lax.dot_general(a, b, (((1,), (1,)), ((), ())), preferred_element_type=F32)


def _tn_dot(a, b):
    return lax.dot_general(a, b, (((0,), (0,)), ((), ())), preferred_element_type=F32)


def _ada_kernel(c_ref, w_ref, b_ref, o_ref):
    c = c_ref[...]
    cond = c * jax.nn.sigmoid(c)
    o_ref[...] = jnp.dot(cond, w_ref[...], preferred_element_type=F32,
                         precision=lax.Precision.HIGHEST) + b_ref[...]


def _ada(c, w_ada, b_ada):
    B, D = c.shape
    n = w_ada.shape[1] // D
    return pl.pallas_call(
        _ada_kernel,
        grid=(n,),
        in_specs=[pl.BlockSpec((B, D), lambda j: (0, 0)),
                  pl.BlockSpec((D, D), lambda j: (0, j)),
                  pl.BlockSpec((1, D), lambda j: (0, j))],
        out_specs=pl.BlockSpec((B, D), lambda j: (0, j)),
        out_shape=jax.ShapeDtypeStruct((B, n * D), F32),
        name="ada",
    )(c, w_ada, b_ada.reshape(1, -1))


_C_Q = 0
_C_K = _C_Q + ATTN_WIDTH
_C_V = _C_K + KV_WIDTH
_C_QKM = _C_V + KV_WIDTH
_C_VM = _C_QKM + 2 * MLSTM_WIDTH
_C_OM = _C_VM + MLSTM_WIDTH
_C_G = _C_OM + MLSTM_WIDTH
_C_END = _C_G + LANES


def _inproj_kernel(x_ref, xp_ref, xn_ref, mod_ref, g_ref, w_ref, gb_ref, qg_ref, kg_ref, cos_ref, sa_ref, sb_ref,
                   bd_ref, cw_ref, cb_ref,
                   q_ref, k_ref, vt_ref, qm_ref, ktm_ref, vm_ref, om_ref, grow_ref, cpad_ref):
    D = D_MODEL
    mod = mod_ref[0]
    sh1 = mod[:, 0:D]
    sc1 = mod[:, D:2 * D]

    def normed(x):
        ms = jnp.mean(x * x, axis=-1, keepdims=True)
        h = (x * lax.rsqrt(ms + EPS)) * g_ref[...]
        return (h * (1.0 + sc1) + sh1).astype(BF16)

    hb = normed(x_ref[0])
    hb_halo = normed(jnp.concatenate([xp_ref[0], xn_ref[0]], axis=0))

    cosf = cos_ref[...]
    sa = sa_ref[...]
    sb = sb_ref[...]
    bd = bd_ref[...]

    def norm_rope(xs, gain, scale):
        ss = jnp.dot((xs * xs).astype(BF16), bd, preferred_element_type=F32)
        xn = (xs * lax.rsqrt(ss + EPS)) * gain
        out = xn * cosf + pltpu.roll(xn, LANES - 16, 1) * sa + pltpu.roll(xn, 16, 1) * sb
        return out * scale

    hd = ATTN_HEAD_DIM
    qkv = jnp.dot(hb, w_ref[:, _C_Q:_C_QKM], preferred_element_type=F32)
    for j in range(ATTN_WIDTH // LANES):
        qs = norm_rope(qkv[:, j * LANES:(j + 1) * LANES], qg_ref[...], Q_SCALE).astype(BF16)
        q_ref[0, 2 * j] = qs[:, :hd]
        q_ref[0, 2 * j + 1] = qs[:, hd:]
    ks = norm_rope(qkv[:, _C_K:_C_K + LANES], kg_ref[...], 1.0).astype(BF16)
    k_ref[0, 0] = ks[:, :hd]
    k_ref[0, 1] = ks[:, hd:]
    vt = qkv[:, _C_V:_C_V + LANES].T.astype(BF16)
    vt_ref[0, 0] = vt[:hd]
    vt_ref[0, 1] = vt[hd:]
    tm = hb.shape[0]
    halo = F32_SUBLANES
    qk_pre = jnp.dot(jnp.concatenate([hb, hb_halo], axis=0), w_ref[:, _C_QKM:_C_VM],
                     preferred_element_type=F32)
    i = pl.program_id(1)
    before = jnp.where(i > 0, qk_pre[tm:tm + halo], 0.0)
    after = jnp.where(i < pl.num_programs(1) - 1, qk_pre[tm + halo:], 0.0)
    rb = LANES
    for j0 in range(0, 2 * MLSTM_WIDTH, LANES):
        cols = slice(j0, j0 + LANES)
        slab = j0 // LANES
        cpad_ref[slab, 0:halo] = before[:, cols]
        cpad_ref[slab, halo:halo + tm] = qk_pre[0:tm, cols]
        cpad_ref[slab, halo + tm:2 * halo + tm] = after[:, cols]
        for r0 in range(0, tm, rb):
            acc = jnp.zeros((rb, LANES), F32) + cb_ref[:, cols]
            for t in range(CONV_WIDTH):
                off = halo + r0 + t - CONV_WIDTH // 2
                acc = acc + cpad_ref[slab, off:off + rb, :] * cw_ref[t:t + 1, cols]
            y = acc * jax.nn.sigmoid(acc)
            if j0 < MLSTM_WIDTH:
                qm_ref[0, r0:r0 + rb, cols] = y.astype(BF16)
            else:
                ktm_ref[0, j0 - MLSTM_WIDTH:j0 - MLSTM_WIDTH + LANES, r0:r0 + rb] = \
                    (y * MLSTM_HEAD_DIM ** -0.5).T.astype(BF16)
    vm_ref[0] = jnp.dot(hb, w_ref[:, _C_VM:_C_OM], preferred_element_type=F32).astype(BF16)
    om_ref[0] = jnp.dot(hb, w_ref[:, _C_OM:_C_G], preferred_element_type=F32)
    gates = jnp.dot(hb, w_ref[:, _C_G:_C_END], preferred_element_type=F32) + gb_ref[...]
    gt = gates.T
    tm = gates.shape[0]
    H = MLSTM_HEADS
    unused = jnp.zeros((GATE_SLOTS - 4, tm), F32)
    for h in range(H):
        grow_ref[0, h] = jnp.concatenate([gt[t * H + h:t * H + h + 1] for t in range(4)] + [unused], axis=0)


def _rope_tables(S):
    rows = S // GRID_W
    row = np.repeat(np.arange(rows, dtype=np.float64), GRID_W)
    col = np.tile(np.arange(GRID_W, dtype=np.float64), rows)
    axis_dim = ATTN_HEAD_DIM // 2
    inv_freq = (ROPE_THETA ** (-np.arange(0, axis_dim, 2, dtype=np.float32) / axis_dim)).astype(np.float32)
    ar = (row[:, None].astype(np.float32) * inv_freq[None, :]).astype(np.float64)
    ac = (col[:, None].astype(np.float32) * inv_freq[None, :]).astype(np.float64)
    cr, sr, cc, sc = np.cos(ar), np.sin(ar), np.cos(ac), np.sin(ac)
    z = np.zeros_like(sr)
    cos_h = np.concatenate([cr, cr, cc, cc], axis=-1)
    sa_h = np.concatenate([-sr, z, -sc, z], axis=-1)
    sb_h = np.concatenate([z, sr, z, sc], axis=-1)
    two = lambda a: jnp.asarray(np.concatenate([a, a], axis=-1), F32)
    return two(cos_h), two(sa_h), two(sb_h)


def _inproj(x, mod3, norm_g, w_in, gate_b, q_g, k_g, conv_w, conv_b):
    B, S, D = x.shape
    tm = TM_IN
    H = MLSTM_HEADS
    w = jnp.pad(w_in.astype(BF16), ((0, 0), (0, _C_END - w_in.shape[1])))
    gb = jnp.pad(gate_b.reshape(1, -1), ((0, 0), (0, LANES - N_GATES)))
    cosf, sa, sb = _rope_tables(S)
    qg2 = jnp.concatenate([q_g, q_g]).reshape(1, LANES)
    kg2 = jnp.concatenate([k_g, k_g]).reshape(1, LANES)
    blk = np.kron(np.eye(LANES // ATTN_HEAD_DIM), np.ones((ATTN_HEAD_DIM, ATTN_HEAD_DIM))) / ATTN_HEAD_DIM
    bd = jnp.asarray(blk, BF16)

    const = lambda shape: pl.BlockSpec(shape, lambda b, i: (0,) * len(shape))
    hd = ATTN_HEAD_DIM
    out_shape = (
        jax.ShapeDtypeStruct((B, ATTN_HEADS, S, hd), BF16),
        jax.ShapeDtypeStruct((B, ATTN_KV_HEADS, S, hd), BF16),
        jax.ShapeDtypeStruct((B, ATTN_KV_HEADS, hd, S), BF16),
        jax.ShapeDtypeStruct((B, S, MLSTM_WIDTH), BF16),
        jax.ShapeDtypeStruct((B, MLSTM_WIDTH, S), BF16),
        jax.ShapeDtypeStruct((B, S, MLSTM_WIDTH), BF16),
        jax.ShapeDtypeStruct((B, S, MLSTM_WIDTH), F32),
        jax.ShapeDtypeStruct((B, H, GATE_SLOTS, S), F32),
    )
    out_specs = (
        pl.BlockSpec((1, ATTN_HEADS, tm, hd), lambda b, i: (b, 0, i, 0)),
        pl.BlockSpec((1, ATTN_KV_HEADS, tm, hd), lambda b, i: (b, 0, i, 0)),
        pl.BlockSpec((1, ATTN_KV_HEADS, hd, tm), lambda b, i: (b, 0, 0, i)),
        pl.BlockSpec((1, tm, MLSTM_WIDTH), lambda b, i: (b, i, 0)),
        pl.BlockSpec((1, MLSTM_WIDTH, tm), lambda b, i: (b, 0, i)),
        pl.BlockSpec((1, tm, MLSTM_WIDTH), lambda b, i: (b, i, 0)),
        pl.BlockSpec((1, tm, MLSTM_WIDTH), lambda b, i: (b, i, 0)),
        pl.BlockSpec((1, H, GATE_SLOTS, tm), lambda b, i: (b, 0, 0, i)),
    )
    halo = F32_SUBLANES
    tiles = tm // halo
    in_specs = [
        pl.BlockSpec((1, tm, D), lambda b, i: (b, i, 0)),
        pl.BlockSpec((1, halo, D), lambda b, i: (b, jnp.maximum(i * tiles - 1, 0), 0)),
        pl.BlockSpec((1, halo, D), lambda b, i: (b, jnp.minimum((i + 1) * tiles, S // halo - 1), 0)),
        pl.BlockSpec((1, 1, 6 * D), lambda b, i: (b, 0, 0)),
        const((1, D)),
        const((D, _C_END)),
        const((1, LANES)),
        const((1, LANES)),
        const((1, LANES)),
        pl.BlockSpec((tm, LANES), lambda b, i: (i, 0)),
        pl.BlockSpec((tm, LANES), lambda b, i: (i, 0)),
        pl.BlockSpec((tm, LANES), lambda b, i: (i, 0)),
        const((LANES, LANES)),
        const((CONV_WIDTH, 2 * MLSTM_WIDTH)),
        const((1, 2 * MLSTM_WIDTH)),
    ]
    return pl.pallas_call(
        _inproj_kernel,
        grid=(B, S // tm),
        in_specs=in_specs,
        out_specs=out_specs,
        out_shape=out_shape,
        scratch_shapes=[
            pltpu.VMEM((2 * MLSTM_WIDTH // LANES, tm + 2 * halo, LANES), F32)],
        compiler_params=pltpu.CompilerParams(
            dimension_semantics=("parallel", "parallel"), vmem_limit_bytes=VMEM_LIMIT),
        name="inproj",
    )(x, x, x, mod3, norm_g.reshape(1, D), w, gb, qg2, kg2, cosf, sa, sb, bd, conv_w, conv_b.reshape(1, -1))


def _attn_kernel(bound_ref, q_ref, k_ref, vt_ref, o_ref, s0_ref, vx_ref):
    G = ATTN_GROUP
    hd = ATTN_HEAD_DIM
    S = k_ref.shape[2]
    R = G * TQ
    nk = S // TK
    nq = S // TQ

    vx_ref[0:hd] = vt_ref[0, 0]
    extra = lax.broadcasted_iota(jnp.int32, (VX_ROWS - hd, S), 0)
    vx_ref[hd:VX_ROWS] = (extra == 0).astype(BF16)

    def load_q(qi):
        r0 = pl.multiple_of(qi * TQ, TQ)
        return jnp.concatenate([q_ref[0, g, pl.ds(r0, TQ), :] for g in range(G)], axis=0)

    def scores(qb, kj, tk=TK):
        kb = k_ref[0, 0, kj * tk:(kj + 1) * tk, :]
        return _nt_dot(kb, qb)

    def update(kj, s, m, acc):
        mn = jnp.maximum(m, jnp.max(s, axis=0, keepdims=True))
        alpha = jnp.exp2(m - mn)
        p = jnp.exp2(s - mn).astype(BF16)
        vb = vx_ref[:, kj * TK:(kj + 1) * TK]
        acc = alpha * acc + jnp.dot(vb, p, preferred_element_type=F32)
        return mn, acc

    def finish(qi, num, den):
        o = (num / den).astype(BF16)
        r0 = pl.multiple_of(qi * TQ, TQ)
        for g in range(G):
            o_ref[0, g * hd:(g + 1) * hd, pl.ds(r0, TQ)] = o[:, g * TQ:(g + 1) * TQ]

    bound = bound_ref[0]

    @pl.when(bound <= SAFE_SCORE_BOUND)
    def _():
        def qblock(qi, carry):
            qb = load_q(qi)
            acc = jnp.zeros((hd, R), F32)
            l = jnp.zeros((F32_SUBLANES, R), F32)
            tk = TK_BOUNDED
            ahead = [scores(qb, kj, tk) for kj in range(SCORE_TILES_AHEAD)]
            for kj in range(S // tk):
                if kj + SCORE_TILES_AHEAD < S // tk:
                    ahead.append(scores(qb, kj + SCORE_TILES_AHEAD, tk))
                p = jnp.exp2(ahead[kj] - bound)
                l = l + jnp.sum(p.reshape(tk // F32_SUBLANES, F32_SUBLANES, R), axis=0)
                acc = acc + jnp.dot(vt_ref[0, 0, :, kj * tk:(kj + 1) * tk], p.astype(BF16),
                                    preferred_element_type=F32)
            finish(qi, acc, jnp.sum(l, axis=0, keepdims=True))
            return carry

        lax.fori_loop(0, nq, qblock, 0, unroll=2)

    @pl.when(bound > SAFE_SCORE_BOUND)
    def _():
        s0_ref[...] = scores(load_q(0), 0)

        def qblock(qi, carry):
            qb = load_q(qi)
            ss = [scores(qb, kj) for kj in range(1, nk)]
            s_next = scores(load_q(jnp.minimum(qi + 1, nq - 1)), 0)
            m = jnp.full((1, R), -jnp.inf, F32)
            acc = jnp.zeros((VX_ROWS, R), F32)
            m, acc = update(0, s0_ref[...], m, acc)
            for kj in range(1, nk):
                m, acc = update(kj, ss[kj - 1], m, acc)
            s0_ref[...] = s_next
            finish(qi, acc[0:hd], acc[hd:hd + 1])
            return carry

        lax.fori_loop(0, nq, qblock, 0)


def _attention(q, k, vt, q_g, k_g):
    B, _, S, hd = q.shape
    G = ATTN_GROUP
    bound = (SCORE_BOUND_MARGIN * hd * Q_SCALE * jnp.max(jnp.abs(q_g)) * jnp.max(jnp.abs(k_g))).reshape(1)
    return pl.pallas_call(
        _attn_kernel,
        grid=(B, ATTN_KV_HEADS),
        in_specs=[
            pl.BlockSpec(memory_space=pltpu.SMEM),
            pl.BlockSpec((1, G, S, hd), lambda b, kv: (b, kv, 0, 0)),
            pl.BlockSpec((1, 1, S, hd), lambda b, kv: (b, kv, 0, 0)),
            pl.BlockSpec((1, 1, hd, S), lambda b, kv: (b, kv, 0, 0)),
        ],
        out_specs=pl.BlockSpec((1, G * hd, S), lambda b, kv: (b, kv, 0)),
        out_shape=jax.ShapeDtypeStruct((B, ATTN_WIDTH, S), BF16),
        scratch_shapes=[pltpu.VMEM((TK, G * TQ), F32),
                        pltpu.VMEM((VX_ROWS, S), BF16)],
        compiler_params=pltpu.CompilerParams(
            dimension_semantics=("parallel", "parallel"), vmem_limit_bytes=VMEM_LIMIT),
        name="attn",
    )(bound.astype(F32), q, k, vt)


def _log_sigmoid(x):
    return jnp.minimum(x, 0.0) - jnp.log(1.0 + jnp.exp(-jnp.abs(x)))


def _mlstm_kernel(qs_ref, kt_ref, v_ref, o_ref, grow_ref, ng_ref, y_ref,
                  rows_ref, col_ref, hf_ref, hb_ref, ct_ref):
    S = v_ref.shape[1]
    L = CHUNK
    assert L == LANES, "chunk tiles are square (L, 128) tiles"
    nc = S // L
    Dh = MLSTM_HEAD_DIM
    RB = 512

    ri = lax.broadcasted_iota(jnp.int32, (L, L), 0)
    ci = lax.broadcasted_iota(jnp.int32, (L, L), 1)
    lower = ci <= ri
    upper = ri <= ci
    tril_b = lower.astype(BF16)
    triu_b = upper.astype(BF16)

    def split3(x):
        t1 = x.astype(BF16).astype(F32)
        e1 = x - t1
        t2 = e1.astype(BF16).astype(F32)
        t3 = (e1 - t2).astype(BF16).astype(F32)
        return t1, t2, t3

    NR = GATE_SLOTS * nc
    graw = jnp.concatenate([grow_ref[0, 0, :, c * L:(c + 1) * L] for c in range(nc)], axis=0)
    tri2 = jnp.concatenate([triu_b, tril_b], axis=1)
    ps = sum(jnp.dot(t.astype(BF16), tri2, preferred_element_type=F32)
             for t in split3(_log_sigmoid(graw) * LOG2E))
    pre, suf = ps[:, :L], ps[:, L:]
    rj = lax.broadcasted_iota(jnp.int32, (NR, L), 0) & (GATE_SLOTS - 1)
    lane = lax.broadcasted_iota(jnp.int32, (NR, L), 1)
    is_fw = rj == 0
    is_r = (rj & 5) == 0
    rr = graw * LOG2E - jnp.where(is_fw, pltpu.roll(pre, NR - 1, 0), pltpu.roll(suf, NR - 1, 0))
    x = jnp.where(is_r, rr, -jnp.inf)
    sh = 1
    while sh < L:
        fwd = jnp.where(lane >= sh, pltpu.roll(x, sh, 1), -jnp.inf)
        bwd = jnp.where(lane < L - sh, pltpu.roll(x, L - sh, 1), -jnp.inf)
        x = jnp.maximum(x, jnp.where(is_fw, fwd, bwd))
        sh *= 2
    quant = jnp.where(is_r, rr, jnp.where(rj == 1, pre, jnp.where(rj == 3, suf, jnp.where(
        (rj & 5) == 4, pltpu.roll(x, 4, 0), 0.0))))
    src_rows = (1, 0, 4, 3, 2, 6)
    ncol = LANES
    kk = lax.broadcasted_iota(jnp.int32, (4 * GATE_SLOTS, ncol), 0)
    blk = lax.broadcasted_iota(jnp.int32, (4 * GATE_SLOTS, ncol), 1)
    want = jnp.full((4 * GATE_SLOTS, ncol), -1, jnp.int32)
    for q, j in enumerate(src_rows):
        want = jnp.where(blk == q, j, want)
    sel = ((kk & (GATE_SLOTS - 1)) == jnp.where(kk < 3 * GATE_SLOTS, want, -1)).astype(BF16)
    terms = split3(quant)
    zeros8 = jnp.zeros((GATE_SLOTS, L), F32)
    stacked = []
    for c in range(nc):
        rows = slice(c * GATE_SLOTS, (c + 1) * GATE_SLOTS)
        rows_ref[:, c * L:(c + 1) * L] = quant[rows]
        stacked.append(jnp.concatenate([t[rows] for t in terms] + [zeros8], axis=0))
    col_ref[...] = _tn_dot(jnp.concatenate(stacked, axis=1).astype(BF16), sel)

    ones_blk = jnp.ones((L, LANES), BF16)
    ct_ref[...] = jnp.zeros_like(ct_ref)

    def direction(c, m, d_idx, mask, g_row, out_ref):
        c0 = pl.multiple_of(c * L, L)
        qc = qs_ref[0, pl.ds(c0, L), :]
        ktc = kt_ref[0, :, pl.ds(c0, L)]
        vc = v_ref[0, pl.ds(c0, L), :]
        r_row = rows_ref[2 * d_idx:2 * d_idx + 1, pl.ds(c0, L)]
        colblk = col_ref[pl.ds(c0, L), :]
        b = jnp.broadcast_to(colblk[:, 3 * d_idx:3 * d_idx + 1], (L, LANES))
        r = jnp.broadcast_to(colblk[:, 3 * d_idx + 1:3 * d_idx + 2], (L, LANES))
        cmax = jnp.broadcast_to(colblk[:, 3 * d_idx + 2:3 * d_idx + 3], (L, LANES))
        rmax = cmax[g_row:g_row + 1]
        p = jnp.exp2(jnp.where(mask, r_row - cmax, -jnp.inf))
        s = (jnp.dot(qc, ktc, preferred_element_type=F32) * p).astype(BF16)
        intra = jnp.dot(s, jnp.concatenate([vc, ones_blk], axis=1), preferred_element_type=F32)
        w = jnp.exp2(r - rmax)
        wv = jnp.concatenate([(w * vc.astype(F32)).astype(BF16), w.astype(BF16)], axis=1)
        upd = jnp.dot(ktc, wv, preferred_element_type=F32)
        mc = jnp.maximum(m, cmax)
        w_inter = jnp.exp2(m - mc)
        w_intra = jnp.exp2(cmax - mc)
        ct = ct_ref[d_idx]
        inter = jnp.dot(qc, ct.astype(BF16), preferred_element_type=F32)
        num = w_inter * inter[:, :Dh] + w_intra * intra[:, :Dh]
        den = w_inter * inter[:, Dh:] + w_intra * intra[:, Dh:]
        out_ref[pl.ds(c0, L), :] = num / jnp.maximum(jnp.abs(den), jnp.exp2(-(b + mc)))
        g = b[g_row:g_row + 1]
        mu = jnp.maximum(m, rmax)
        decay = jnp.exp2(m - mu)
        gain = jnp.exp2(rmax - mu)
        ct_ref[d_idx] = jnp.concatenate([decay, decay], axis=1) * ct \
            + jnp.concatenate([gain, gain], axis=1) * upd
        return g + mu

    def step(j, carry):
        m_f, m_b = carry
        m_f = direction(j, m_f, 0, lower, L - 1, hf_ref)
        m_b = direction(nc - 1 - j, m_b, 1, upper, 0, hb_ref)
        return m_f, m_b

    zero_row = jnp.zeros((1, LANES), F32)
    lax.fori_loop(0, nc, step, (zero_row, zero_row), unroll=8)

    ng = ng_ref[...]
    for r in range(S // RB):
        sl = slice(r * RB, (r + 1) * RB)
        hsum = hf_ref[sl] + hb_ref[sl]
        hn = hsum * lax.rsqrt(jnp.mean(hsum * hsum, axis=-1, keepdims=True) + EPS)
        hn = hn * ng
        y_ref[0, sl] = (hn * jax.nn.sigmoid(o_ref[0, sl])).astype(BF16)


def _mlstm(qm, ktm, vm, om, grow, norm_g):
    B, S, _ = vm.shape
    H = MLSTM_HEADS
    Dh = MLSTM_HEAD_DIM
    seq = pl.BlockSpec((1, S, Dh), lambda b, h: (b, 0, h))
    return pl.pallas_call(
        _mlstm_kernel,
        grid=(B, H),
        in_specs=[
            seq,
            pl.BlockSpec((1, Dh, S), lambda b, h: (b, h, 0)),
            seq, seq,
            pl.BlockSpec((1, 1, GATE_SLOTS, S), lambda b, h: (b, h, 0, 0)),
            pl.BlockSpec((1, Dh), lambda b, h: (0, h)),
        ],
        out_specs=pl.BlockSpec((1, S, Dh), lambda b, h: (b, 0, h)),
        out_shape=jax.ShapeDtypeStruct((B, S, MLSTM_WIDTH), BF16),
        scratch_shapes=[
            pltpu.VMEM((GATE_SLOTS, S), F32),
            pltpu.VMEM((S, LANES), F32),
            pltpu.VMEM((S, Dh), F32),
            pltpu.VMEM((S, Dh), F32),
            pltpu.VMEM((2, Dh, 2 * Dh), F32),
        ],
        compiler_params=pltpu.CompilerParams(
            dimension_semantics=("parallel", "parallel"), vmem_limit_bytes=VMEM_LIMIT),
        name="mlstm",
    )(qm, ktm, vm, om, grow, norm_g.reshape(1, -1))


def _outffn_kernel(x_ref, mod_ref, yat_ref, ym_ref, wo_ref, gf_ref, wg_ref, wu_ref, wd_ref, fg_ref, o_ref):
    D = D_MODEL
    x = x_ref[0]
    mod = mod_ref[0]
    g1 = mod[:, 2 * D:3 * D]
    sh2 = mod[:, 3 * D:4 * D]
    sc2 = mod[:, 4 * D:5 * D]
    g2 = mod[:, 5 * D:6 * D]
    mix = _tn_dot(yat_ref[0], wo_ref[0:ATTN_WIDTH, :]) \
        + jnp.dot(ym_ref[0], wo_ref[ATTN_WIDTH:MIX_WIDTH, :], preferred_element_type=F32)
    x1 = x + g1 * mix
    ms = jnp.mean(x1 * x1, axis=-1, keepdims=True)
    h = (x1 * lax.rsqrt(ms + EPS)) * gf_ref[...]
    hb = (h * (1.0 + sc2) + sh2).astype(BF16)
    gate = jnp.dot(hb, wg_ref[...], preferred_element_type=F32)
    up = jnp.dot(hb, wu_ref[...], preferred_element_type=F32)
    act = (gate * jax.nn.sigmoid(gate) * up).astype(BF16)
    ffn = jnp.dot(act, wd_ref[...], preferred_element_type=F32)
    x2 = x1 + g2 * ffn
    ms2 = jnp.mean(x2 * x2, axis=-1, keepdims=True)
    o_ref[0] = (x2 * lax.rsqrt(ms2 + EPS)) * fg_ref[...]


def _outffn(x, mod3, yat, ym, w_out, norm_ffn_g, w_gate, w_up, w_down, final_g):
    B, S, D = x.shape
    tm = TM_FFN
    F = w_gate.shape[1]
    const = lambda shape: pl.BlockSpec(shape, lambda b, i: (0,) * len(shape), pipeline_mode=pl.Buffered(1))
    return pl.pallas_call(
        _outffn_kernel,
        grid=(B, S // tm),
        in_specs=[
            pl.BlockSpec((1, tm, D), lambda b, i: (b, i, 0)),
            pl.BlockSpec((1, 1, 6 * D), lambda b, i: (b, 0, 0)),
            pl.BlockSpec((1, ATTN_WIDTH, tm), lambda b, i: (b, 0, i)),
            pl.BlockSpec((1, tm, MLSTM_WIDTH), lambda b, i: (b, i, 0)),
            const((MIX_WIDTH, D)),
            const((1, D)),
            const((D, F)),
            const((D, F)),
            const((F, D)),
            const((1, D)),
        ],
        out_specs=pl.BlockSpec((1, tm, D), lambda b, i: (b, i, 0)),
        out_shape=jax.ShapeDtypeStruct((B, S, D), F32),
        compiler_params=pltpu.CompilerParams(
            dimension_semantics=("parallel", "parallel"), vmem_limit_bytes=VMEM_LIMIT),
        name="outffn",
    )(x, mod3, yat, ym, w_out.astype(BF16), norm_ffn_g.reshape(1, D), w_gate.astype(BF16),
      w_up.astype(BF16), w_down.astype(BF16), final_g.reshape(1, D))


def kernel(x, c, w_ada, b_ada, norm_mix_g, w_in, q_norm_g, k_norm_g, conv_w, conv_b, gate_b,
           mlstm_norm_g, w_out, norm_ffn_g, w_gate, w_up, w_down, final_norm_g):
    assert w_ada.shape[0] == 1, "the final RMSNorm is fused into the single layer's last kernel"
    B = x.shape[0]
    l = 0
    mod3 = _ada(c, w_ada[l], b_ada[l]).reshape(B, 1, -1)
    q, k, vt, qm, ktm, vm, om, grow = _inproj(x, mod3, norm_mix_g[l], w_in[l], gate_b[l],
                                              q_norm_g[l], k_norm_g[l], conv_w[l], conv_b[l])
    yat = _attention(q, k, vt, q_norm_g[l], k_norm_g[l])
    ym = _mlstm(qm, ktm, vm, om, grow, mlstm_norm_g[l])
    return _outffn(x, mod3, yat, ym, w_out[l], norm_ffn_g[l], w_gate[l], w_up[l], w_down[l], final_norm_g)
```

```python
import functools

import jax
import jax.numpy as jnp
import numpy as np
from jax import lax
from jax.experimental import pallas as pl
from jax.experimental.pallas import tpu as pltpu

F32 = jnp.float32
BF16 = jnp.bfloat16

D_MODEL = 1024
GRID_W = 64
EPS = 1e-6
ATTN_HEADS = 8
ATTN_KV_HEADS = 2
ATTN_GROUP = ATTN_HEADS // ATTN_KV_HEADS
ATTN_HEAD_DIM = 64
ATTN_WIDTH = ATTN_HEADS * ATTN_HEAD_DIM
KV_WIDTH = ATTN_KV_HEADS * ATTN_HEAD_DIM
ROPE_THETA = 10000.0
MLSTM_HEADS = 4
MLSTM_HEAD_DIM = 128
MLSTM_WIDTH = MLSTM_HEADS * MLSTM_HEAD_DIM
CONV_WIDTH = 5
N_GATES = 4 * MLSTM_HEADS
MIX_WIDTH = ATTN_WIDTH + MLSTM_WIDTH
FFN_HIDDEN = 256 * ((8 * D_MODEL + 3 * 256 - 1) // (3 * 256))

LANES = 128
GATE_SLOTS = 8

TM_IN = 512
TM_FFN = 512
TQ = 128
TK = 256
F32_SUBLANES = 8
BF16_SUBLANES = 16
LOG2E = float(np.log2(np.e))
VX_ROWS = ATTN_HEAD_DIM + BF16_SUBLANES
CHUNK = 128
Q_SCALE = ATTN_HEAD_DIM ** -0.5 * float(np.log2(np.e))
VMEM_LIMIT = 56 * 1024 * 1024
SAFE_SCORE_BOUND = 50.0
TK_BOUNDED = 256
SCORE_TILES_AHEAD = 2
SCORE_BOUND_MARGIN = 1.02


def _nt_dot(a, b):
    return lax.dot_general(a, b, (((1,), (1,)), ((), ())), preferred_element_type=F32)


def _tn_dot(a, b):
    return lax.dot_general(a, b, (((0,), (0,)), ((), ())), preferred_element_type=F32)


def _ada_kernel(c_ref, w_ref, b_ref, o_ref):
    c = c_ref[...]
    cond = c * jax.nn.sigmoid(c)
    o_ref[...] = jnp.dot(cond, w_ref[...], preferred_element_type=F32,
                         precision=lax.Precision.HIGHEST) + b_ref[...]


def _ada(c, w_ada, b_ada):
    B, D = c.shape
    n = w_ada.shape[1] // D
    return pl.pallas_call(
        _ada_kernel,
        grid=(n,),
        in_specs=[pl.BlockSpec((B, D), lambda j: (0, 0)),
                  pl.BlockSpec((D, D), lambda j: (0, j)),
                  pl.BlockSpec((1, D), lambda j: (0, j))],
        out_specs=pl.BlockSpec((B, D), lambda j: (0, j)),
        out_shape=jax.ShapeDtypeStruct((B, n * D), F32),
        name="ada",
    )(c, w_ada, b_ada.reshape(1, -1))


_C_Q = 0
_C_K = _C_Q + ATTN_WIDTH
_C_V = _C_K + KV_WIDTH
_C_QKM = _C_V + KV_WIDTH
_C_VM = _C_QKM + 2 * MLSTM_WIDTH
_C_OM = _C_VM + MLSTM_WIDTH
_C_G = _C_OM + MLSTM_WIDTH
_C_END = _C_G + LANES


def _inproj_kernel(x_ref, xp_ref, xn_ref, mod_ref, g_ref, w_ref, gb_ref, qg_ref, kg_ref, cos_ref, sa_ref, sb_ref,
                   bd_ref, cw_ref, cb_ref,
                   q_ref, k_ref, vt_ref, qm_ref, ktm_ref, vm_ref, om_ref, grow_ref, cpad_ref):
    D = D_MODEL
    mod = mod_ref[0]
    sh1 = mod[:, 0:D]
    sc1 = mod[:, D:2 * D]

    def normed(x):
        ms = jnp.mean(x * x, axis=-1, keepdims=True)
        h = (x * lax.rsqrt(ms + EPS)) * g_ref[...]
        return (h * (1.0 + sc1) + sh1).astype(BF16)

    hb = normed(x_ref[0])
    hb_halo = normed(jnp.concatenate([xp_ref[0], xn_ref[0]], axis=0))

    cosf = cos_ref[...]
    sa = sa_ref[...]
    sb = sb_ref[...]
    bd = bd_ref[...]

    def norm_rope(xs, gain, scale):
        ss = jnp.dot((xs * xs).astype(BF16), bd, preferred_element_type=F32)
        xn = (xs * lax.rsqrt(ss + EPS)) * gain
        out = xn * cosf + pltpu.roll(xn, LANES - 16, 1) * sa + pltpu.roll(xn, 16, 1) * sb
        return out * scale

    hd = ATTN_HEAD_DIM
    qkv = jnp.dot(hb, w_ref[:, _C_Q:_C_QKM], preferred_element_type=F32)
    for j in range(ATTN_WIDTH // LANES):
        qt = norm_rope(qkv[:, j * LANES:(j + 1) * LANES], qg_ref[...], Q_SCALE).T.astype(BF16)
        q_ref[0, 2 * j] = qt[:hd]
        q_ref[0, 2 * j + 1] = qt[hd:]
    ks = norm_rope(qkv[:, _C_K:_C_K + LANES], kg_ref[...], 1.0).astype(BF16)
    k_ref[0, 0] = ks[:, :hd]
    k_ref[0, 1] = ks[:, hd:]
    vt = qkv[:, _C_V:_C_V + LANES].T.astype(BF16)
    vt_ref[0, 0] = vt[:hd]
    vt_ref[0, 1] = vt[hd:]
    tm = hb.shape[0]
    halo = F32_SUBLANES
    qk_pre = jnp.dot(jnp.concatenate([hb, hb_halo], axis=0), w_ref[:, _C_QKM:_C_VM],
                     preferred_element_type=F32)
    i = pl.program_id(1)
    before = jnp.where(i > 0, qk_pre[tm:tm + halo], 0.0)
    after = jnp.where(i < pl.num_programs(1) - 1, qk_pre[tm + halo:], 0.0)
    rb = LANES
    for j0 in range(0, 2 * MLSTM_WIDTH, LANES):
        cols = slice(j0, j0 + LANES)
        slab = j0 // LANES
        cpad_ref[slab, 0:halo] = before[:, cols]
        cpad_ref[slab, halo:halo + tm] = qk_pre[0:tm, cols]
        cpad_ref[slab, halo + tm:2 * halo + tm] = after[:, cols]
        for r0 in range(0, tm, rb):
            acc = jnp.zeros((rb, LANES), F32) + cb_ref[:, cols]
            for t in range(CONV_WIDTH):
                off = halo + r0 + t - CONV_WIDTH // 2
                acc = acc + cpad_ref[slab, off:off + rb, :] * cw_ref[t:t + 1, cols]
            y = acc * jax.nn.sigmoid(acc)
            if j0 < MLSTM_WIDTH:
                qm_ref[0, r0:r0 + rb, cols] = y.astype(BF16)
            else:
                ktm_ref[0, j0 - MLSTM_WIDTH:j0 - MLSTM_WIDTH + LANES, r0:r0 + rb] = \
                    (y * MLSTM_HEAD_DIM ** -0.5).T.astype(BF16)
    vm_ref[0] = jnp.dot(hb, w_ref[:, _C_VM:_C_OM], preferred_element_type=F32).astype(BF16)
    om_ref[0] = jnp.dot(hb, w_ref[:, _C_OM:_C_G], preferred_element_type=F32)
    gates = jnp.dot(hb, w_ref[:, _C_G:_C_END], preferred_element_type=F32) + gb_ref[...]
    gt = gates.T
    tm = gates.shape[0]
    H = MLSTM_HEADS
    unused = jnp.zeros((GATE_SLOTS - 4, tm), F32)
    for h in range(H):
        grow_ref[0, h] = jnp.concatenate([gt[t * H + h:t * H + h + 1] for t in range(4)] + [unused], axis=0)


def _rope_tables(S):
    rows = S // GRID_W
    row = np.repeat(np.arange(rows, dtype=np.float64), GRID_W)
    col = np.tile(np.arange(GRID_W, dtype=np.float64), rows)
    axis_dim = ATTN_HEAD_DIM // 2
    inv_freq = (ROPE_THETA ** (-np.arange(0, axis_dim, 2, dtype=np.float32) / axis_dim)).astype(np.float32)
    ar = (row[:, None].astype(np.float32) * inv_freq[None, :]).astype(np.float64)
    ac = (col[:, None].astype(np.float32) * inv_freq[None, :]).astype(np.float64)
    cr, sr, cc, sc = np.cos(ar), np.sin(ar), np.cos(ac), np.sin(ac)
    z = np.zeros_like(sr)
    cos_h = np.concatenate([cr, cr, cc, cc], axis=-1)
    sa_h = np.concatenate([-sr, z, -sc, z], axis=-1)
    sb_h = np.concatenate([z, sr, z, sc], axis=-1)
    two = lambda a: jnp.asarray(np.concatenate([a, a], axis=-1), F32)
    return two(cos_h), two(sa_h), two(sb_h)


def _inproj(x, mod3, norm_g, w_in, gate_b, q_g, k_g, conv_w, conv_b):
    B, S, D = x.shape
    tm = TM_IN
    H = MLSTM_HEADS
    w = jnp.pad(w_in.astype(BF16), ((0, 0), (0, _C_END - w_in.shape[1])))
    gb = jnp.pad(gate_b.reshape(1, -1), ((0, 0), (0, LANES - N_GATES)))
    cosf, sa, sb = _rope_tables(S)
    qg2 = jnp.concatenate([q_g, q_g]).reshape(1, LANES)
    kg2 = jnp.concatenate([k_g, k_g]).reshape(1, LANES)
    blk = np.kron(np.eye(LANES // ATTN_HEAD_DIM), np.ones((ATTN_HEAD_DIM, ATTN_HEAD_DIM))) / ATTN_HEAD_DIM
    bd = jnp.asarray(blk, BF16)

    const = lambda shape: pl.BlockSpec(shape, lambda b, i: (0,) * len(shape))
    hd = ATTN_HEAD_DIM
    out_shape = (
        jax.ShapeDtypeStruct((B, ATTN_HEADS, hd, S), BF16),
        jax.ShapeDtypeStruct((B, ATTN_KV_HEADS, S, hd), BF16),
        jax.ShapeDtypeStruct((B, ATTN_KV_HEADS, hd, S), BF16),
        jax.ShapeDtypeStruct((B, S, MLSTM_WIDTH), BF16),
        jax.ShapeDtypeStruct((B, MLSTM_WIDTH, S), BF16),
        jax.ShapeDtypeStruct((B, S, MLSTM_WIDTH), BF16),
        jax.ShapeDtypeStruct((B, S, MLSTM_WIDTH), F32),
        jax.ShapeDtypeStruct((B, H, GATE_SLOTS, S), F32),
    )
    out_specs = (
        pl.BlockSpec((1, ATTN_HEADS, hd, tm), lambda b, i: (b, 0, 0, i)),
        pl.BlockSpec((1, ATTN_KV_HEADS, tm, hd), lambda b, i: (b, 0, i, 0)),
        pl.BlockSpec((1, ATTN_KV_HEADS, hd, tm), lambda b, i: (b, 0, 0, i)),
        pl.BlockSpec((1, tm, MLSTM_WIDTH), lambda b, i: (b, i, 0)),
        pl.BlockSpec((1, MLSTM_WIDTH, tm), lambda b, i: (b, 0, i)),
        pl.BlockSpec((1, tm, MLSTM_WIDTH), lambda b, i: (b, i, 0)),
        pl.BlockSpec((1, tm, MLSTM_WIDTH), lambda b, i: (b, i, 0)),
        pl.BlockSpec((1, H, GATE_SLOTS, tm), lambda b, i: (b, 0, 0, i)),
    )
    halo = F32_SUBLANES
    tiles = tm // halo
    in_specs = [
        pl.BlockSpec((1, tm, D), lambda b, i: (b, i, 0)),
        pl.BlockSpec((1, halo, D), lambda b, i: (b, jnp.maximum(i * tiles - 1, 0), 0)),
        pl.BlockSpec((1, halo, D), lambda b, i: (b, jnp.minimum((i + 1) * tiles, S // halo - 1), 0)),
        pl.BlockSpec((1, 1, 6 * D), lambda b, i: (b, 0, 0)),
        const((1, D)),
        const((D, _C_END)),
        const((1, LANES)),
        const((1, LANES)),
        const((1, LANES)),
        pl.BlockSpec((tm, LANES), lambda b, i: (i, 0)),
        pl.BlockSpec((tm, LANES), lambda b, i: (i, 0)),
        pl.BlockSpec((tm, LANES), lambda b, i: (i, 0)),
        const((LANES, LANES)),
        const((CONV_WIDTH, 2 * MLSTM_WIDTH)),
        const((1, 2 * MLSTM_WIDTH)),
    ]
    return pl.pallas_call(
        _inproj_kernel,
        grid=(B, S // tm),
        in_specs=in_specs,
        out_specs=out_specs,
        out_shape=out_shape,
        scratch_shapes=[
            pltpu.VMEM((2 * MLSTM_WIDTH // LANES, tm + 2 * halo, LANES), F32)],
        compiler_params=pltpu.CompilerParams(
            dimension_semantics=("parallel", "parallel"), vmem_limit_bytes=VMEM_LIMIT),
        name="inproj",
    )(x, x, x, mod3, norm_g.reshape(1, D), w, gb, qg2, kg2, cosf, sa, sb, bd, conv_w, conv_b.reshape(1, -1))


def _attn_kernel(bound_ref, q_ref, k_ref, vt_ref, o_ref, s0_ref, vx_ref):
    G = ATTN_GROUP
    hd = ATTN_HEAD_DIM
    S = k_ref.shape[2]
    R = G * TQ
    nk = S // TK
    nq = S // TQ

    vx_ref[0:hd] = vt_ref[0, 0]
    extra = lax.broadcasted_iota(jnp.int32, (VX_ROWS - hd, S), 0)
    vx_ref[hd:VX_ROWS] = (extra == 0).astype(BF16)

    def load_q(qi):
        r0 = pl.multiple_of(qi * TQ, TQ)
        return jnp.concatenate([q_ref[0, g, :, pl.ds(r0, TQ)] for g in range(G)], axis=1)

    def scores(qb, kj, tk=TK):
        kb = k_ref[0, 0, kj * tk:(kj + 1) * tk, :]
        return jnp.dot(kb, qb, preferred_element_type=F32)

    def update(kj, s, m, acc):
        mn = jnp.maximum(m, jnp.max(s, axis=0, keepdims=True))
        alpha = jnp.exp2(m - mn)
        p = jnp.exp2(s - mn).astype(BF16)
        vb = vx_ref[:, kj * TK:(kj + 1) * TK]
        acc = alpha * acc + jnp.dot(vb, p, preferred_element_type=F32)
        return mn, acc

    def finish(qi, num, den):
        o = (num / den).astype(BF16)
        r0 = pl.multiple_of(qi * TQ, TQ)
        for g in range(G):
            o_ref[0, g * hd:(g + 1) * hd, pl.ds(r0, TQ)] = o[:, g * TQ:(g + 1) * TQ]

    bound = bound_ref[0]

    @pl.when(bound <= SAFE_SCORE_BOUND)
    def _():
        def qblock(qi, carry):
            qb = load_q(qi)
            acc = jnp.zeros((hd, R), F32)
            l = jnp.zeros((F32_SUBLANES, R), F32)
            tk = TK_BOUNDED
            ahead = [scores(qb, kj, tk) for kj in range(SCORE_TILES_AHEAD)]
            for kj in range(S // tk):
                if kj + SCORE_TILES_AHEAD < S // tk:
                    ahead.append(scores(qb, kj + SCORE_TILES_AHEAD, tk))
                p = jnp.exp2(ahead[kj] - bound)
                l = l + jnp.sum(p.reshape(tk // F32_SUBLANES, F32_SUBLANES, R), axis=0)
                acc = acc + jnp.dot(vt_ref[0, 0, :, kj * tk:(kj + 1) * tk], p.astype(BF16),
                                    preferred_element_type=F32)
            finish(qi, acc, jnp.sum(l, axis=0, keepdims=True))
            return carry

        lax.fori_loop(0, nq, qblock, 0, unroll=2)

    @pl.when(bound > SAFE_SCORE_BOUND)
    def _():
        s0_ref[...] = scores(load_q(0), 0)

        def qblock(qi, carry):
            qb = load_q(qi)
            ss = [scores(qb, kj) for kj in range(1, nk)]
            s_next = scores(load_q(jnp.minimum(qi + 1, nq - 1)), 0)
            m = jnp.full((1, R), -jnp.inf, F32)
            acc = jnp.zeros((VX_ROWS, R), F32)
            m, acc = update(0, s0_ref[...], m, acc)
            for kj in range(1, nk):
                m, acc = update(kj, ss[kj - 1], m, acc)
            s0_ref[...] = s_next
            finish(qi, acc[0:hd], acc[hd:hd + 1])
            return carry

        lax.fori_loop(0, nq, qblock, 0)


def _attention(q, k, vt, q_g, k_g):
    B, _, hd, S = q.shape
    G = ATTN_GROUP
    bound = (SCORE_BOUND_MARGIN * hd * Q_SCALE * jnp.max(jnp.abs(q_g)) * jnp.max(jnp.abs(k_g))).reshape(1)
    return pl.pallas_call(
        _attn_kernel,
        grid=(B, ATTN_KV_HEADS),
        in_specs=[
            pl.BlockSpec(memory_space=pltpu.SMEM),
            pl.BlockSpec((1, G, hd, S), lambda b, kv: (b, kv, 0, 0)),
            pl.BlockSpec((1, 1, S, hd), lambda b, kv: (b, kv, 0, 0)),
            pl.BlockSpec((1, 1, hd, S), lambda b, kv: (b, kv, 0, 0)),
        ],
        out_specs=pl.BlockSpec((1, G * hd, S), lambda b, kv: (b, kv, 0)),
        out_shape=jax.ShapeDtypeStruct((B, ATTN_WIDTH, S), BF16),
        scratch_shapes=[pltpu.VMEM((TK, G * TQ), F32),
                        pltpu.VMEM((VX_ROWS, S), BF16)],
        compiler_params=pltpu.CompilerParams(
            dimension_semantics=("parallel", "parallel"), vmem_limit_bytes=VMEM_LIMIT),
        name="attn",
    )(bound.astype(F32), q, k, vt)


def _log_sigmoid(x):
    return jnp.minimum(x, 0.0) - jnp.log(1.0 + jnp.exp(-jnp.abs(x)))


def _mlstm_kernel(qs_ref, kt_ref, v_ref, o_ref, grow_ref, ng_ref, y_ref,
                  rows_ref, col_ref, hf_ref, hb_ref, ct_ref):
    S = v_ref.shape[1]
    L = CHUNK
    assert L == LANES, "chunk tiles are square (L, 128) tiles"
    nc = S // L
    Dh = MLSTM_HEAD_DIM
    RB = 512

    ri = lax.broadcasted_iota(jnp.int32, (L, L), 0)
    ci = lax.broadcasted_iota(jnp.int32, (L, L), 1)
    lower = ci <= ri
    upper = ri <= ci
    tril_b = lower.astype(BF16)
    triu_b = upper.astype(BF16)

    def split3(x):
        t1 = x.astype(BF16).astype(F32)
        e1 = x - t1
        t2 = e1.astype(BF16).astype(F32)
        t3 = (e1 - t2).astype(BF16).astype(F32)
        return t1, t2, t3

    NR = GATE_SLOTS * nc
    graw = jnp.concatenate([grow_ref[0, 0, :, c * L:(c + 1) * L] for c in range(nc)], axis=0)
    tri2 = jnp.concatenate([triu_b, tril_b], axis=1)
    ps = sum(jnp.dot(t.astype(BF16), tri2, preferred_element_type=F32)
             for t in split3(_log_sigmoid(graw) * LOG2E))
    pre, suf = ps[:, :L], ps[:, L:]
    rj = lax.broadcasted_iota(jnp.int32, (NR, L), 0) & (GATE_SLOTS - 1)
    lane = lax.broadcasted_iota(jnp.int32, (NR, L), 1)
    is_fw = rj == 0
    is_r = (rj & 5) == 0
    rr = graw * LOG2E - jnp.where(is_fw, pltpu.roll(pre, NR - 1, 0), pltpu.roll(suf, NR - 1, 0))
    x = jnp.where(is_r, rr, -jnp.inf)
    sh = 1
    while sh < L:
        fwd = jnp.where(lane >= sh, pltpu.roll(x, sh, 1), -jnp.inf)
        bwd = jnp.where(lane < L - sh, pltpu.roll(x, L - sh, 1), -jnp.inf)
        x = jnp.maximum(x, jnp.where(is_fw, fwd, bwd))
        sh *= 2
    quant = jnp.where(is_r, rr, jnp.where(rj == 1, pre, jnp.where(rj == 3, suf, jnp.where(
        (rj & 5) == 4, pltpu.roll(x, 4, 0), 0.0))))
    src_rows = (1, 0, 4, 3, 2, 6)
    ncol = LANES
    kk = lax.broadcasted_iota(jnp.int32, (4 * GATE_SLOTS, ncol), 0)
    blk = lax.broadcasted_iota(jnp.int32, (4 * GATE_SLOTS, ncol), 1)
    want = jnp.full((4 * GATE_SLOTS, ncol), -1, jnp.int32)
    for q, j in enumerate(src_rows):
        want = jnp.where(blk == q, j, want)
    sel = ((kk & (GATE_SLOTS - 1)) == jnp.where(kk < 3 * GATE_SLOTS, want, -1)).astype(BF16)
    terms = split3(quant)
    zeros8 = jnp.zeros((GATE_SLOTS, L), F32)
    stacked = []
    for c in range(nc):
        rows = slice(c * GATE_SLOTS, (c + 1) * GATE_SLOTS)
        rows_ref[:, c * L:(c + 1) * L] = quant[rows]
        stacked.append(jnp.concatenate([t[rows] for t in terms] + [zeros8], axis=0))
    col_ref[...] = _tn_dot(jnp.concatenate(stacked, axis=1).astype(BF16), sel)

    ones_blk = jnp.ones((L, LANES), BF16)
    ct_ref[...] = jnp.zeros_like(ct_ref)

    def direction(c, m, d_idx, mask, g_row, out_ref):
        c0 = pl.multiple_of(c * L, L)
        qc = qs_ref[0, pl.ds(c0, L), :]
        ktc = kt_ref[0, :, pl.ds(c0, L)]
        vc = v_ref[0, pl.ds(c0, L), :]
        r_row = rows_ref[2 * d_idx:2 * d_idx + 1, pl.ds(c0, L)]
        colblk = col_ref[pl.ds(c0, L), :]
        b = jnp.broadcast_to(colblk[:, 3 * d_idx:3 * d_idx + 1], (L, LANES))
        r = jnp.broadcast_to(colblk[:, 3 * d_idx + 1:3 * d_idx + 2], (L, LANES))
        cmax = jnp.broadcast_to(colblk[:, 3 * d_idx + 2:3 * d_idx + 3], (L, LANES))
        rmax = cmax[g_row:g_row + 1]
        p = jnp.exp2(jnp.where(mask, r_row - cmax, -jnp.inf))
        s = (jnp.dot(qc, ktc, preferred_element_type=F32) * p).astype(BF16)
        intra = jnp.dot(s, jnp.concatenate([vc, ones_blk], axis=1), preferred_element_type=F32)
        w = jnp.exp2(r - rmax)
        wv = jnp.concatenate([(w * vc.astype(F32)).astype(BF16), w.astype(BF16)], axis=1)
        upd = jnp.dot(ktc, wv, preferred_element_type=F32)
        mc = jnp.maximum(m, cmax)
        w_inter = jnp.exp2(m - mc)
        w_intra = jnp.exp2(cmax - mc)
        ct = ct_ref[d_idx]
        inter = jnp.dot(qc, ct.astype(BF16), preferred_element_type=F32)
        num = w_inter * inter[:, :Dh] + w_intra * intra[:, :Dh]
        den = w_inter * inter[:, Dh:] + w_intra * intra[:, Dh:]
        out_ref[pl.ds(c0, L), :] = num / jnp.maximum(jnp.abs(den), jnp.exp2(-(b + mc)))
        g = b[g_row:g_row + 1]
        mu = jnp.maximum(m, rmax)
        decay = jnp.exp2(m - mu)
        gain = jnp.exp2(rmax - mu)
        ct_ref[d_idx] = jnp.concatenate([decay, decay], axis=1) * ct \
            + jnp.concatenate([gain, gain], axis=1) * upd
        return g + mu

    def step(j, carry):
        m_f, m_b = carry
        m_f = direction(j, m_f, 0, lower, L - 1, hf_ref)
        m_b = direction(nc - 1 - j, m_b, 1, upper, 0, hb_ref)
        return m_f, m_b

    zero_row = jnp.zeros((1, LANES), F32)
    lax.fori_loop(0, nc, step, (zero_row, zero_row), unroll=8)

    ng = ng_ref[...]
    for r in range(S // RB):
        sl = slice(r * RB, (r + 1) * RB)
        hsum = hf_ref[sl] + hb_ref[sl]
        hn = hsum * lax.rsqrt(jnp.mean(hsum * hsum, axis=-1, keepdims=True) + EPS)
        hn = hn * ng
        y_ref[0, sl] = (hn * jax.nn.sigmoid(o_ref[0, sl])).astype(BF16)


def _mlstm(qm, ktm, vm, om, grow, norm_g):
    B, S, _ = vm.shape
    H = MLSTM_HEADS
    Dh = MLSTM_HEAD_DIM
    seq = pl.BlockSpec((1, S, Dh), lambda b, h: (b, 0, h))
    return pl.pallas_call(
        _mlstm_kernel,
        grid=(B, H),
        in_specs=[
            seq,
            pl.BlockSpec((1, Dh, S), lambda b, h: (b, h, 0)),
            seq, seq,
            pl.BlockSpec((1, 1, GATE_SLOTS, S), lambda b, h: (b, h, 0, 0)),
            pl.BlockSpec((1, Dh), lambda b, h: (0, h)),
        ],
        out_specs=pl.BlockSpec((1, S, Dh), lambda b, h: (b, 0, h)),
        out_shape=jax.ShapeDtypeStruct((B, S, MLSTM_WIDTH), BF16),
        scratch_shapes=[
            pltpu.VMEM((GATE_SLOTS, S), F32),
            pltpu.VMEM((S, LANES), F32),
            pltpu.VMEM((S, Dh), F32),
            pltpu.VMEM((S, Dh), F32),
            pltpu.VMEM((2, Dh, 2 * Dh), F32),
        ],
        compiler_params=pltpu.CompilerParams(
            dimension_semantics=("parallel", "parallel"), vmem_limit_bytes=VMEM_LIMIT),
        name="mlstm",
    )(qm, ktm, vm, om, grow, norm_g.reshape(1, -1))


def _outffn_kernel(x_ref, mod_ref, yat_ref, ym_ref, wo_ref, gf_ref, wg_ref, wu_ref, wd_ref, fg_ref, o_ref):
    D = D_MODEL
    x = x_ref[0]
    mod = mod_ref[0]
    g1 = mod[:, 2 * D:3 * D]
    sh2 = mod[:, 3 * D:4 * D]
    sc2 = mod[:, 4 * D:5 * D]
    g2 = mod[:, 5 * D:6 * D]
    mix = _tn_dot(yat_ref[0], wo_ref[0:ATTN_WIDTH, :]) \
        + jnp.dot(ym_ref[0], wo_ref[ATTN_WIDTH:MIX_WIDTH, :], preferred_element_type=F32)
    x1 = x + g1 * mix
    ms = jnp.mean(x1 * x1, axis=-1, keepdims=True)
    h = (x1 * lax.rsqrt(ms + EPS)) * gf_ref[...]
    hb = (h * (1.0 + sc2) + sh2).astype(BF16)
    gate = jnp.dot(hb, wg_ref[...], preferred_element_type=F32)
    up = jnp.dot(hb, wu_ref[...], preferred_element_type=F32)
    act = (gate * jax.nn.sigmoid(gate) * up).astype(BF16)
    ffn = jnp.dot(act, wd_ref[...], preferred_element_type=F32)
    x2 = x1 + g2 * ffn
    ms2 = jnp.mean(x2 * x2, axis=-1, keepdims=True)
    o_ref[0] = (x2 * lax.rsqrt(ms2 + EPS)) * fg_ref[...]


def _outffn(x, mod3, yat, ym, w_out, norm_ffn_g, w_gate, w_up, w_down, final_g):
    B, S, D = x.shape
    tm = TM_FFN
    F = w_gate.shape[1]
    const = lambda shape: pl.BlockSpec(shape, lambda b, i: (0,) * len(shape), pipeline_mode=pl.Buffered(1))
    return pl.pallas_call(
        _outffn_kernel,
        grid=(B, S // tm),
        in_specs=[
            pl.BlockSpec((1, tm, D), lambda b, i: (b, i, 0)),
            pl.BlockSpec((1, 1, 6 * D), lambda b, i: (b, 0, 0)),
            pl.BlockSpec((1, ATTN_WIDTH, tm), lambda b, i: (b, 0, i)),
            pl.BlockSpec((1, tm, MLSTM_WIDTH), lambda b, i: (b, i, 0)),
            const((MIX_WIDTH, D)),
            const((1, D)),
            const((D, F)),
            const((D, F)),
            const((F, D)),
            const((1, D)),
        ],
        out_specs=pl.BlockSpec((1, tm, D), lambda b, i: (b, i, 0)),
        out_shape=jax.ShapeDtypeStruct((B, S, D), F32),
        compiler_params=pltpu.CompilerParams(
            dimension_semantics=("parallel", "parallel"), vmem_limit_bytes=VMEM_LIMIT),
        name="outffn",
    )(x, mod3, yat, ym, w_out.astype(BF16), norm_ffn_g.reshape(1, D), w_gate.astype(BF16),
      w_up.astype(BF16), w_down.astype(BF16), final_g.reshape(1, D))


def kernel(x, c, w_ada, b_ada, norm_mix_g, w_in, q_norm_g, k_norm_g, conv_w, conv_b, gate_b,
           mlstm_norm_g, w_out, norm_ffn_g, w_gate, w_up, w_down, final_norm_g):
    assert w_ada.shape[0] == 1, "the final RMSNorm is fused into the single layer's last kernel"
    B = x.shape[0]
    l = 0
    mod3 = _ada(c, w_ada[l], b_ada[l]).reshape(B, 1, -1)
    q, k, vt, qm, ktm, vm, om, grow = _inproj(x, mod3, norm_mix_g[l], w_in[l], gate_b[l],
                                              q_norm_g[l], k_norm_g[l], conv_w[l], conv_b[l])
    yat = _attention(q, k, vt, q_norm_g[l], k_norm_g[l])
    ym = _mlstm(qm, ktm, vm, om, grow, mlstm_norm_g[l])
    return _outffn(x, mod3, yat, ym, w_out[l], norm_ffn_g[l], w_gate[l], w_up[l], w_down[l], final_norm_g)
```

```python
import functools

import jax
import jax.numpy as jnp
import numpy as np
from jax import lax
from jax.experimental import pallas as pl
from jax.experimental.pallas import tpu as pltpu

F32 = jnp.float32
BF16 = jnp.bfloat16

D_MODEL = 1024
GRID_W = 64
EPS = 1e-6
ATTN_HEADS = 8
ATTN_KV_HEADS = 2
ATTN_GROUP = ATTN_HEADS // ATTN_KV_HEADS
ATTN_HEAD_DIM = 64
ATTN_WIDTH = ATTN_HEADS * ATTN_HEAD_DIM
KV_WIDTH = ATTN_KV_HEADS * ATTN_HEAD_DIM
ROPE_THETA = 10000.0
MLSTM_HEADS = 4
MLSTM_HEAD_DIM = 128
MLSTM_WIDTH = MLSTM_HEADS * MLSTM_HEAD_DIM
CONV_WIDTH = 5
N_GATES = 4 * MLSTM_HEADS
MIX_WIDTH = ATTN_WIDTH + MLSTM_WIDTH
FFN_HIDDEN = 256 * ((8 * D_MODEL + 3 * 256 - 1) // (3 * 256))

LANES = 128
GATE_SLOTS = 8

TM_IN = 512
TM_FFN = 512
TQ = 128
TK = 256
F32_SUBLANES = 8
BF16_SUBLANES = 16
LOG2E = float(np.log2(np.e))
VX_ROWS = ATTN_HEAD_DIM + BF16_SUBLANES
CHUNK = 128
Q_SCALE = ATTN_HEAD_DIM ** -0.5 * float(np.log2(np.e))
VMEM_LIMIT = 56 * 1024 * 1024
SAFE_SCORE_BOUND = 50.0
TK_BOUNDED = 256
SCORE_TILES_AHEAD = 2
SCORE_BOUND_MARGIN = 1.02


def _nt_dot(a, b):
    return lax.dot_general(a, b, (((1,), (1,)), ((), ())), preferred_element_type=F32)


def _tn_dot(a, b):
    return lax.dot_general(a, b, (((0,), (0,)), ((), ())), preferred_element_type=F32)


def _ada_kernel(c_ref, w_ref, b_ref, o_ref):
    c = c_ref[...]
    cond = c * jax.nn.sigmoid(c)
    o_ref[...] = jnp.dot(cond, w_ref[...], preferred_element_type=F32,
                         precision=lax.Precision.HIGHEST) + b_ref[...]


def _ada(c, w_ada, b_ada):
    B, D = c.shape
    n = w_ada.shape[1] // D
    return pl.pallas_call(
        _ada_kernel,
        grid=(n,),
        in_specs=[pl.BlockSpec((B, D), lambda j: (0, 0)),
                  pl.BlockSpec((D, D), lambda j: (0, j)),
                  pl.BlockSpec((1, D), lambda j: (0, j))],
        out_specs=pl.BlockSpec((B, D), lambda j: (0, j)),
        out_shape=jax.ShapeDtypeStruct((B, n * D), F32),
        name="ada",
    )(c, w_ada, b_ada.reshape(1, -1))


_C_Q = 0
_C_K = _C_Q + ATTN_WIDTH
_C_V = _C_K + KV_WIDTH
_C_QKM = _C_V + KV_WIDTH
_C_VM = _C_QKM + 2 * MLSTM_WIDTH
_C_OM = _C_VM + MLSTM_WIDTH
_C_G = _C_OM + MLSTM_WIDTH
_C_END = _C_G + LANES


def _inproj_kernel(x_ref, xp_ref, xn_ref, mod_ref, g_ref, w_ref, gb_ref, qg_ref, kg_ref, cos_ref, sa_ref, sb_ref,
                   bd_ref, cw_ref, cb_ref,
                   q_ref, k_ref, vt_ref, qm_ref, ktm_ref, vm_ref, om_ref, grow_ref, cpad_ref):
    D = D_MODEL
    mod = mod_ref[0]
    sh1 = mod[:, 0:D]
    sc1 = mod[:, D:2 * D]

    def normed(x):
        ms = jnp.mean(x * x, axis=-1, keepdims=True)
        h = (x * lax.rsqrt(ms + EPS)) * g_ref[...]
        return (h * (1.0 + sc1) + sh1).astype(BF16)

    hb = normed(x_ref[0])
    hb_halo = normed(jnp.concatenate([xp_ref[0], xn_ref[0]], axis=0))

    cosf = cos_ref[...]
    sa = sa_ref[...]
    sb = sb_ref[...]
    bd = bd_ref[...]

    def norm_rope(xs, gain, scale):
        ss = jnp.dot((xs * xs).astype(BF16), bd, preferred_element_type=F32)
        xn = (xs * lax.rsqrt(ss + EPS)) * gain
        out = xn * cosf + pltpu.roll(xn, LANES - 16, 1) * sa + pltpu.roll(xn, 16, 1) * sb
        return out * scale

    hd = ATTN_HEAD_DIM
    qkv = jnp.dot(hb, w_ref[:, _C_Q:_C_QKM], preferred_element_type=F32)
    for j in range(ATTN_WIDTH // LANES):
        qt = norm_rope(qkv[:, j * LANES:(j + 1) * LANES], qg_ref[...], Q_SCALE).T.astype(BF16)
        q_ref[0, 2 * j] = qt[:hd]
        q_ref[0, 2 * j + 1] = qt[hd:]
    ks = norm_rope(qkv[:, _C_K:_C_K + LANES], kg_ref[...], 1.0).astype(BF16)
    k_ref[0, 0] = ks[:, :hd]
    k_ref[0, 1] = ks[:, hd:]
    vt = qkv[:, _C_V:_C_V + LANES].T.astype(BF16)
    vt_ref[0, 0] = vt[:hd]
    vt_ref[0, 1] = vt[hd:]
    tm = hb.shape[0]
    halo = F32_SUBLANES
    qk_pre = jnp.dot(jnp.concatenate([hb, hb_halo], axis=0), w_ref[:, _C_QKM:_C_VM],
                     preferred_element_type=F32)
    i = pl.program_id(1)
    before = jnp.where(i > 0, qk_pre[tm:tm + halo], 0.0)
    after = jnp.where(i < pl.num_programs(1) - 1, qk_pre[tm + halo:], 0.0)
    rb = LANES
    for j0 in range(0, 2 * MLSTM_WIDTH, LANES):
        cols = slice(j0, j0 + LANES)
        slab = j0 // LANES
        cpad_ref[slab, 0:halo] = before[:, cols]
        cpad_ref[slab, halo:halo + tm] = qk_pre[0:tm, cols]
        cpad_ref[slab, halo + tm:2 * halo + tm] = after[:, cols]
        for r0 in range(0, tm, rb):
            acc = jnp.zeros((rb, LANES), F32) + cb_ref[:, cols]
            for t in range(CONV_WIDTH):
                off = halo + r0 + t - CONV_WIDTH // 2
                acc = acc + cpad_ref[slab, off:off + rb, :] * cw_ref[t:t + 1, cols]
            y = acc * jax.nn.sigmoid(acc)
            if j0 < MLSTM_WIDTH:
                qm_ref[0, r0:r0 + rb, cols] = y.astype(BF16)
            else:
                ktm_ref[0, j0 - MLSTM_WIDTH:j0 - MLSTM_WIDTH + LANES, r0:r0 + rb] = \
                    (y * MLSTM_HEAD_DIM ** -0.5).T.astype(BF16)
    vm_ref[0] = jnp.dot(hb, w_ref[:, _C_VM:_C_OM], preferred_element_type=F32).astype(BF16)
    om_ref[0] = jnp.dot(hb, w_ref[:, _C_OM:_C_G], preferred_element_type=F32)
    gates = jnp.dot(hb, w_ref[:, _C_G:_C_END], preferred_element_type=F32) + gb_ref[...]
    gt = gates.T
    tm = gates.shape[0]
    H = MLSTM_HEADS
    unused = jnp.zeros((GATE_SLOTS - 4, tm), F32)
    for h in range(H):
        grow_ref[0, h] = jnp.concatenate([gt[t * H + h:t * H + h + 1] for t in range(4)] + [unused], axis=0)


def _rope_tables(S):
    rows = S // GRID_W
    row = np.repeat(np.arange(rows, dtype=np.float64), GRID_W)
    col = np.tile(np.arange(GRID_W, dtype=np.float64), rows)
    axis_dim = ATTN_HEAD_DIM // 2
    inv_freq = (ROPE_THETA ** (-np.arange(0, axis_dim, 2, dtype=np.float32) / axis_dim)).astype(np.float32)
    ar = (row[:, None].astype(np.float32) * inv_freq[None, :]).astype(np.float64)
    ac = (col[:, None].astype(np.float32) * inv_freq[None, :]).astype(np.float64)
    cr, sr, cc, sc = np.cos(ar), np.sin(ar), np.cos(ac), np.sin(ac)
    z = np.zeros_like(sr)
    cos_h = np.concatenate([cr, cr, cc, cc], axis=-1)
    sa_h = np.concatenate([-sr, z, -sc, z], axis=-1)
    sb_h = np.concatenate([z, sr, z, sc], axis=-1)
    two = lambda a: jnp.asarray(np.concatenate([a, a], axis=-1), F32)
    return two(cos_h), two(sa_h), two(sb_h)


def _inproj(x, mod3, norm_g, w_in, gate_b, q_g, k_g, conv_w, conv_b):
    B, S, D = x.shape
    tm = TM_IN
    H = MLSTM_HEADS
    w = jnp.pad(w_in.astype(BF16), ((0, 0), (0, _C_END - w_in.shape[1])))
    gb = jnp.pad(gate_b.reshape(1, -1), ((0, 0), (0, LANES - N_GATES)))
    cosf, sa, sb = _rope_tables(S)
    qg2 = jnp.concatenate([q_g, q_g]).reshape(1, LANES)
    kg2 = jnp.concatenate([k_g, k_g]).reshape(1, LANES)
    blk = np.kron(np.eye(LANES // ATTN_HEAD_DIM), np.ones((ATTN_HEAD_DIM, ATTN_HEAD_DIM))) / ATTN_HEAD_DIM
    bd = jnp.asarray(blk, BF16)

    const = lambda shape: pl.BlockSpec(shape, lambda b, i: (0,) * len(shape))
    hd = ATTN_HEAD_DIM
    out_shape = (
        jax.ShapeDtypeStruct((B, ATTN_HEADS, hd, S), BF16),
        jax.ShapeDtypeStruct((B, ATTN_KV_HEADS, S, hd), BF16),
        jax.ShapeDtypeStruct((B, ATTN_KV_HEADS, hd, S), BF16),
        jax.ShapeDtypeStruct((B, S, MLSTM_WIDTH), BF16),
        jax.ShapeDtypeStruct((B, MLSTM_WIDTH, S), BF16),
        jax.ShapeDtypeStruct((B, S, MLSTM_WIDTH), BF16),
        jax.ShapeDtypeStruct((B, S, MLSTM_WIDTH), F32),
        jax.ShapeDtypeStruct((B, H, GATE_SLOTS, S), F32),
    )
    out_specs = (
        pl.BlockSpec((1, ATTN_HEADS, hd, tm), lambda b, i: (b, 0, 0, i)),
        pl.BlockSpec((1, ATTN_KV_HEADS, tm, hd), lambda b, i: (b, 0, i, 0)),
        pl.BlockSpec((1, ATTN_KV_HEADS, hd, tm), lambda b, i: (b, 0, 0, i)),
        pl.BlockSpec((1, tm, MLSTM_WIDTH), lambda b, i: (b, i, 0)),
        pl.BlockSpec((1, MLSTM_WIDTH, tm), lambda b, i: (b, 0, i)),
        pl.BlockSpec((1, tm, MLSTM_WIDTH), lambda b, i: (b, i, 0)),
        pl.BlockSpec((1, tm, MLSTM_WIDTH), lambda b, i: (b, i, 0)),
        pl.BlockSpec((1, H, GATE_SLOTS, tm), lambda b, i: (b, 0, 0, i)),
    )
    halo = F32_SUBLANES
    tiles = tm // halo
    in_specs = [
        pl.BlockSpec((1, tm, D), lambda b, i: (b, i, 0)),
        pl.BlockSpec((1, halo, D), lambda b, i: (b, jnp.maximum(i * tiles - 1, 0), 0)),
        pl.BlockSpec((1, halo, D), lambda b, i: (b, jnp.minimum((i + 1) * tiles, S // halo - 1), 0)),
        pl.BlockSpec((1, 1, 6 * D), lambda b, i: (b, 0, 0)),
        const((1, D)),
        const((D, _C_END)),
        const((1, LANES)),
        const((1, LANES)),
        const((1, LANES)),
        pl.BlockSpec((tm, LANES), lambda b, i: (i, 0)),
        pl.BlockSpec((tm, LANES), lambda b, i: (i, 0)),
        pl.BlockSpec((tm, LANES), lambda b, i: (i, 0)),
        const((LANES, LANES)),
        const((CONV_WIDTH, 2 * MLSTM_WIDTH)),
        const((1, 2 * MLSTM_WIDTH)),
    ]
    return pl.pallas_call(
        _inproj_kernel,
        grid=(B, S // tm),
        in_specs=in_specs,
        out_specs=out_specs,
        out_shape=out_shape,
        scratch_shapes=[
            pltpu.VMEM((2 * MLSTM_WIDTH // LANES, tm + 2 * halo, LANES), F32)],
        compiler_params=pltpu.CompilerParams(
            dimension_semantics=("parallel", "parallel"), vmem_limit_bytes=VMEM_LIMIT),
        name="inproj",
    )(x, x, x, mod3, norm_g.reshape(1, D), w, gb, qg2, kg2, cosf, sa, sb, bd, conv_w, conv_b.reshape(1, -1))


def _attn_kernel(bound_ref, q_ref, k_ref, vt_ref, o_ref, s0_ref, vx_ref):
    G = ATTN_GROUP
    hd = ATTN_HEAD_DIM
    S = k_ref.shape[2]
    R = G * TQ
    nk = S // TK
    nq = S // TQ

    vx_ref[0:hd] = vt_ref[0, 0]
    extra = lax.broadcasted_iota(jnp.int32, (VX_ROWS - hd, S), 0)
    vx_ref[hd:VX_ROWS] = (extra == 0).astype(BF16)

    def load_q(qi):
        r0 = pl.multiple_of(qi * TQ, TQ)
        return jnp.concatenate([q_ref[0, g, :, pl.ds(r0, TQ)] for g in range(G)], axis=1)

    def scores(qb, kj, tk=TK):
        kb = k_ref[0, 0, kj * tk:(kj + 1) * tk, :]
        return jnp.dot(kb, qb, preferred_element_type=F32)

    def update(kj, s, m, acc):
        mn = jnp.maximum(m, jnp.max(s, axis=0, keepdims=True))
        alpha = jnp.exp2(m - mn)
        p = jnp.exp2(s - mn).astype(BF16)
        vb = vx_ref[:, kj * TK:(kj + 1) * TK]
        acc = alpha * acc + jnp.dot(vb, p, preferred_element_type=F32)
        return mn, acc

    def finish(qi, num, den):
        o = (num / den).astype(BF16)
        r0 = pl.multiple_of(qi * TQ, TQ)
        for g in range(G):
            o_ref[0, g * hd:(g + 1) * hd, pl.ds(r0, TQ)] = o[:, g * TQ:(g + 1) * TQ]

    bound = bound_ref[0]

    @pl.when(bound <= SAFE_SCORE_BOUND)
    def _():
        def qblock(qi, carry):
            qb = load_q(qi)
            acc = jnp.zeros((hd, R), F32)
            l = jnp.zeros((F32_SUBLANES, R), F32)
            tk = TK_BOUNDED
            ahead = [scores(qb, kj, tk) for kj in range(SCORE_TILES_AHEAD)]
            for kj in range(S // tk):
                if kj + SCORE_TILES_AHEAD < S // tk:
                    ahead.append(scores(qb, kj + SCORE_TILES_AHEAD, tk))
                p = jnp.exp2(ahead[kj] - bound)
                l = l + jnp.sum(p.reshape(tk // F32_SUBLANES, F32_SUBLANES, R), axis=0)
                acc = acc + jnp.dot(vt_ref[0, 0, :, kj * tk:(kj + 1) * tk], p.astype(BF16),
                                    preferred_element_type=F32)
            finish(qi, acc, jnp.sum(l, axis=0, keepdims=True))
            return carry

        lax.fori_loop(0, nq, qblock, 0, unroll=2)

    @pl.when(bound > SAFE_SCORE_BOUND)
    def _():
        s0_ref[...] = scores(load_q(0), 0)

        def qblock(qi, carry):
            qb = load_q(qi)
            ss = [scores(qb, kj) for kj in range(1, nk)]
            s_next = scores(load_q(jnp.minimum(qi + 1, nq - 1)), 0)
            m = jnp.full((1, R), -jnp.inf, F32)
            acc = jnp.zeros((VX_ROWS, R), F32)
            m, acc = update(0, s0_ref[...], m, acc)
            for kj in range(1, nk):
                m, acc = update(kj, ss[kj - 1], m, acc)
            s0_ref[...] = s_next
            finish(qi, acc[0:hd], acc[hd:hd + 1])
            return carry

        lax.fori_loop(0, nq, qblock, 0)


def _attention(q, k, vt, q_g, k_g):
    B, _, hd, S = q.shape
    G = ATTN_GROUP
    bound = (SCORE_BOUND_MARGIN * hd * Q_SCALE * jnp.max(jnp.abs(q_g)) * jnp.max(jnp.abs(k_g))).reshape(1)
    return pl.pallas_call(
        _attn_kernel,
        grid=(B, ATTN_KV_HEADS),
        in_specs=[
            pl.BlockSpec(memory_space=pltpu.SMEM),
            pl.BlockSpec((1, G, hd, S), lambda b, kv: (b, kv, 0, 0)),
            pl.BlockSpec((1, 1, S, hd), lambda b, kv: (b, kv, 0, 0)),
            pl.BlockSpec((1, 1, hd, S), lambda b, kv: (b, kv, 0, 0)),
        ],
        out_specs=pl.BlockSpec((1, G * hd, S), lambda b, kv: (b, kv, 0)),
        out_shape=jax.ShapeDtypeStruct((B, ATTN_WIDTH, S), BF16),
        scratch_shapes=[pltpu.VMEM((TK, G * TQ), F32),
                        pltpu.VMEM((VX_ROWS, S), BF16)],
        compiler_params=pltpu.CompilerParams(
            dimension_semantics=("parallel", "parallel"), vmem_limit_bytes=VMEM_LIMIT),
        name="attn",
    )(bound.astype(F32), q, k, vt)


def _log_sigmoid(x):
    return jnp.minimum(x, 0.0) - jnp.log(1.0 + jnp.exp(-jnp.abs(x)))


def _mlstm_kernel(qs_ref, kt_ref, v_ref, o_ref, grow_ref, ng_ref, y_ref,
                  rows_ref, col_ref, hf_ref, hb_ref, ct_ref):
    S = v_ref.shape[1]
    L = CHUNK
    assert L == LANES, "chunk tiles are square (L, 128) tiles"
    nc = S // L
    Dh = MLSTM_HEAD_DIM
    RB = 512

    ri = lax.broadcasted_iota(jnp.int32, (L, L), 0)
    ci = lax.broadcasted_iota(jnp.int32, (L, L), 1)
    lower = ci <= ri
    upper = ri <= ci
    tril_b = lower.astype(BF16)
    triu_b = upper.astype(BF16)

    def split3(x):
        t1 = x.astype(BF16).astype(F32)
        e1 = x - t1
        t2 = e1.astype(BF16).astype(F32)
        t3 = (e1 - t2).astype(BF16).astype(F32)
        return t1, t2, t3

    NR = GATE_SLOTS * nc
    graw = jnp.concatenate([grow_ref[0, 0, :, c * L:(c + 1) * L] for c in range(nc)], axis=0)
    tri2 = jnp.concatenate([triu_b, tril_b], axis=1)
    ps = sum(jnp.dot(t.astype(BF16), tri2, preferred_element_type=F32)
             for t in split3(_log_sigmoid(graw) * LOG2E))
    pre, suf = ps[:, :L], ps[:, L:]
    rj = lax.broadcasted_iota(jnp.int32, (NR, L), 0) & (GATE_SLOTS - 1)
    lane = lax.broadcasted_iota(jnp.int32, (NR, L), 1)
    is_fw = rj == 0
    is_r = (rj & 5) == 0
    rr = graw * LOG2E - jnp.where(is_fw, pltpu.roll(pre, NR - 1, 0), pltpu.roll(suf, NR - 1, 0))
    x = jnp.where(is_r, rr, -jnp.inf)
    sh = 1
    while sh < L:
        fwd = jnp.where(lane >= sh, pltpu.roll(x, sh, 1), -jnp.inf)
        bwd = jnp.where(lane < L - sh, pltpu.roll(x, L - sh, 1), -jnp.inf)
        x = jnp.maximum(x, jnp.where(is_fw, fwd, bwd))
        sh *= 2
    quant = jnp.where(is_r, rr, jnp.where(rj == 1, pre, jnp.where(rj == 3, suf, jnp.where(
        (rj & 5) == 4, pltpu.roll(x, 4, 0), 0.0))))
    src_rows = (1, 0, 4, 3, 2, 6)
    ncol = LANES
    kk = lax.broadcasted_iota(jnp.int32, (4 * GATE_SLOTS, ncol), 0)
    blk = lax.broadcasted_iota(jnp.int32, (4 * GATE_SLOTS, ncol), 1)
    want = jnp.full((4 * GATE_SLOTS, ncol), -1, jnp.int32)
    for q, j in enumerate(src_rows):
        want = jnp.where(blk == q, j, want)
    sel = ((kk & (GATE_SLOTS - 1)) == jnp.where(kk < 3 * GATE_SLOTS, want, -1)).astype(BF16)
    terms = split3(quant)
    zeros8 = jnp.zeros((GATE_SLOTS, L), F32)
    stacked = []
    for c in range(nc):
        rows = slice(c * GATE_SLOTS, (c + 1) * GATE_SLOTS)
        rows_ref[:, c * L:(c + 1) * L] = quant[rows]
        stacked.append(jnp.concatenate([t[rows] for t in terms] + [zeros8], axis=0))
    col_ref[...] = _tn_dot(jnp.concatenate(stacked, axis=1).astype(BF16), sel)

    ones_blk = jnp.ones((L, LANES), BF16)
    ct_ref[...] = jnp.zeros_like(ct_ref)

    def direction(c, m, d_idx, mask, g_row, out_ref):
        c0 = pl.multiple_of(c * L, L)
        qc = qs_ref[0, pl.ds(c0, L), :]
        ktc = kt_ref[0, :, pl.ds(c0, L)]
        vc = v_ref[0, pl.ds(c0, L), :]
        r_row = rows_ref[2 * d_idx:2 * d_idx + 1, pl.ds(c0, L)]
        colblk = col_ref[pl.ds(c0, L), :]
        b = jnp.broadcast_to(colblk[:, 3 * d_idx:3 * d_idx + 1], (L, LANES))
        r = jnp.broadcast_to(colblk[:, 3 * d_idx + 1:3 * d_idx + 2], (L, LANES))
        cmax = jnp.broadcast_to(colblk[:, 3 * d_idx + 2:3 * d_idx + 3], (L, LANES))
        rmax = cmax[g_row:g_row + 1]
        p = jnp.exp2(jnp.where(mask, r_row - cmax, -jnp.inf))
        s = (jnp.dot(qc, ktc, preferred_element_type=F32) * p).astype(BF16)
        intra = jnp.dot(s, jnp.concatenate([vc, ones_blk], axis=1), preferred_element_type=F32)
        w = jnp.exp2(r - rmax)
        wv = jnp.concatenate([(w * vc.astype(F32)).astype(BF16), w.astype(BF16)], axis=1)
        upd = jnp.dot(ktc, wv, preferred_element_type=F32)
        mc = jnp.maximum(m, cmax)
        w_inter = jnp.exp2(m - mc)
        w_intra = jnp.exp2(cmax - mc)
        ct = ct_ref[d_idx]
        inter = jnp.dot(qc, ct.astype(BF16), preferred_element_type=F32)
        num = w_inter * inter[:, :Dh] + w_intra * intra[:, :Dh]
        den = w_inter * inter[:, Dh:] + w_intra * intra[:, Dh:]
        out_ref[pl.ds(c0, L), :] = num / jnp.maximum(jnp.abs(den), jnp.exp2(-(b + mc)))
        g = b[g_row:g_row + 1]
        mu = jnp.maximum(m, rmax)
        decay = jnp.exp2(m - mu)
        gain = jnp.exp2(rmax - mu)
        ct_ref[d_idx] = jnp.concatenate([decay, decay], axis=1) * ct \
            + jnp.concatenate([gain, gain], axis=1) * upd
        return g + mu

    def step(j, carry):
        m_f, m_b = carry
        m_f = direction(j, m_f, 0, lower, L - 1, hf_ref)
        m_b = direction(nc - 1 - j, m_b, 1, upper, 0, hb_ref)
        return m_f, m_b

    zero_row = jnp.zeros((1, LANES), F32)
    lax.fori_loop(0, nc, step, (zero_row, zero_row), unroll=8)

    ng = ng_ref[...]
    for r in range(S // RB):
        sl = slice(r * RB, (r + 1) * RB)
        hsum = hf_ref[sl] + hb_ref[sl]
        hn = hsum * lax.rsqrt(jnp.mean(hsum * hsum, axis=-1, keepdims=True) + EPS)
        hn = hn * ng
        y_ref[0, sl] = (hn * jax.nn.sigmoid(o_ref[0, sl])).astype(BF16)


def _mlstm(qm, ktm, vm, om, grow, norm_g):
    B, S, _ = vm.shape
    H = MLSTM_HEADS
    Dh = MLSTM_HEAD_DIM
    seq = pl.BlockSpec((1, S, Dh), lambda b, h: (b, 0, h))
    return pl.pallas_call(
        _mlstm_kernel,
        grid=(B, H),
        in_specs=[
            seq,
            pl.BlockSpec((1, Dh, S), lambda b, h: (b, h, 0)),
            seq, seq,
            pl.BlockSpec((1, 1, GATE_SLOTS, S), lambda b, h: (b, h, 0, 0)),
            pl.BlockSpec((1, Dh), lambda b, h: (0, h)),
        ],
        out_specs=pl.BlockSpec((1, S, Dh), lambda b, h: (b, 0, h)),
        out_shape=jax.ShapeDtypeStruct((B, S, MLSTM_WIDTH), BF16),
        scratch_shapes=[
            pltpu.VMEM((GATE_SLOTS, S), F32),
            pltpu.VMEM((S, LANES), F32),
            pltpu.VMEM((S, Dh), F32),
            pltpu.VMEM((S, Dh), F32),
            pltpu.VMEM((2, Dh, 2 * Dh), F32),
        ],
        compiler_params=pltpu.CompilerParams(
            dimension_semantics=("parallel", "parallel"), vmem_limit_bytes=VMEM_LIMIT),
        name="mlstm",
    )(qm, ktm, vm, om, grow, norm_g.reshape(1, -1))


def _outffn_kernel(x_ref, mod_ref, yat_ref, ym_ref, wo_ref, gf_ref, wg_ref, wu_ref, wd_ref, fg_ref, o_ref):
    D = D_MODEL
    x = x_ref[0]
    mod = mod_ref[0]
    g1 = mod[:, 2 * D:3 * D]
    sh2 = mod[:, 3 * D:4 * D]
    sc2 = mod[:, 4 * D:5 * D]
    g2 = mod[:, 5 * D:6 * D]
    tm = x.shape[0]
    halves = [slice(0, tm // 2), slice(tm // 2, tm)]
    x1s = []
    for rows in halves:
        mix = _tn_dot(yat_ref[0, :, rows], wo_ref[0:ATTN_WIDTH, :]) \
            + jnp.dot(ym_ref[0, rows, :], wo_ref[ATTN_WIDTH:MIX_WIDTH, :], preferred_element_type=F32)
        x1s.append(x[rows] + g1 * mix)
    for rows, x1 in zip(halves, x1s):
        ms = jnp.mean(x1 * x1, axis=-1, keepdims=True)
        h = (x1 * lax.rsqrt(ms + EPS)) * gf_ref[...]
        hb = (h * (1.0 + sc2) + sh2).astype(BF16)
        gate = jnp.dot(hb, wg_ref[...], preferred_element_type=F32)
        up = jnp.dot(hb, wu_ref[...], preferred_element_type=F32)
        act = (gate * jax.nn.sigmoid(gate) * up).astype(BF16)
        ffn = jnp.dot(act, wd_ref[...], preferred_element_type=F32)
        x2 = x1 + g2 * ffn
        ms2 = jnp.mean(x2 * x2, axis=-1, keepdims=True)
        o_ref[0, rows, :] = (x2 * lax.rsqrt(ms2 + EPS)) * fg_ref[...]


def _outffn(x, mod3, yat, ym, w_out, norm_ffn_g, w_gate, w_up, w_down, final_g):
    B, S, D = x.shape
    tm = TM_FFN
    F = w_gate.shape[1]
    const = lambda shape: pl.BlockSpec(shape, lambda b, i: (0,) * len(shape), pipeline_mode=pl.Buffered(1))
    return pl.pallas_call(
        _outffn_kernel,
        grid=(B, S // tm),
        in_specs=[
            pl.BlockSpec((1, tm, D), lambda b, i: (b, i, 0)),
            pl.BlockSpec((1, 1, 6 * D), lambda b, i: (b, 0, 0)),
            pl.BlockSpec((1, ATTN_WIDTH, tm), lambda b, i: (b, 0, i)),
            pl.BlockSpec((1, tm, MLSTM_WIDTH), lambda b, i: (b, i, 0)),
            const((MIX_WIDTH, D)),
            const((1, D)),
            const((D, F)),
            const((D, F)),
            const((F, D)),
            const((1, D)),
        ],
        out_specs=pl.BlockSpec((1, tm, D), lambda b, i: (b, i, 0)),
        out_shape=jax.ShapeDtypeStruct((B, S, D), F32),
        compiler_params=pltpu.CompilerParams(
            dimension_semantics=("parallel", "parallel"), vmem_limit_bytes=VMEM_LIMIT),
        name="outffn",
    )(x, mod3, yat, ym, w_out.astype(BF16), norm_ffn_g.reshape(1, D), w_gate.astype(BF16),
      w_up.astype(BF16), w_down.astype(BF16), final_g.reshape(1, D))


def kernel(x, c, w_ada, b_ada, norm_mix_g, w_in, q_norm_g, k_norm_g, conv_w, conv_b, gate_b,
           mlstm_norm_g, w_out, norm_ffn_g, w_gate, w_up, w_down, final_norm_g):
    assert w_ada.shape[0] == 1, "the final RMSNorm is fused into the single layer's last kernel"
    B = x.shape[0]
    l = 0
    mod3 = _ada(c, w_ada[l], b_ada[l]).reshape(B, 1, -1)
    q, k, vt, qm, ktm, vm, om, grow = _inproj(x, mod3, norm_mix_g[l], w_in[l], gate_b[l],
                                              q_norm_g[l], k_norm_g[l], conv_w[l], conv_b[l])
    yat = _attention(q, k, vt, q_norm_g[l], k_norm_g[l])
    ym = _mlstm(qm, ktm, vm, om, grow, mlstm_norm_g[l])
    return _outffn(x, mod3, yat, ym, w_out[l], norm_ffn_g[l], w_gate[l], w_up[l], w_down[l], final_norm_g)
```

```python
import jax
import jax.numpy as jnp
import numpy as np
from jax import lax
from jax.experimental import pallas as pl
from jax.experimental.pallas import tpu as pltpu

F32 = jnp.float32
BF16 = jnp.bfloat16

D_MODEL = 1024
GRID_W = 64
EPS = 1e-6
ATTN_HEADS = 8
ATTN_KV_HEADS = 2
ATTN_GROUP = ATTN_HEADS // ATTN_KV_HEADS
ATTN_HEAD_DIM = 64
ATTN_WIDTH = ATTN_HEADS * ATTN_HEAD_DIM
KV_WIDTH = ATTN_KV_HEADS * ATTN_HEAD_DIM
ROPE_THETA = 10000.0
MLSTM_HEADS = 4
MLSTM_HEAD_DIM = 128
MLSTM_WIDTH = MLSTM_HEADS * MLSTM_HEAD_DIM
CONV_WIDTH = 5
N_GATES = 4 * MLSTM_HEADS
MIX_WIDTH = ATTN_WIDTH + MLSTM_WIDTH

LANES = 128
GATE_SLOTS = 8

TM_IN = 1024
TM_FFN = 512
TQ = 128
TK = 256
F32_SUBLANES = 8
BF16_SUBLANES = 16
LOG2E = float(np.log2(np.e))
VX_ROWS = ATTN_HEAD_DIM + BF16_SUBLANES
CHUNK = 128
MLSTM_HEADS_PER_STEP = 1
MLSTM_LOOP_UNROLL = 8
Q_SCALE = ATTN_HEAD_DIM ** -0.5 * float(np.log2(np.e))
VMEM_LIMIT = 56 * 1024 * 1024
SAFE_SCORE_BOUND = 50.0
TK_BOUNDED = 256
SCORE_TILES_AHEAD = 2
SCORE_BOUND_MARGIN = 1.02


def _tn_dot(a, b):
    return lax.dot_general(a, b, (((0,), (0,)), ((), ())), preferred_element_type=F32)


def _ada_kernel(c_ref, w_ref, b_ref, o_ref):
    c = c_ref[...]
    cond = c * jax.nn.sigmoid(c)
    o_ref[...] = jnp.dot(cond, w_ref[...], preferred_element_type=F32,
                         precision=lax.Precision.HIGHEST) + b_ref[...]


def _ada(c, w_ada, b_ada):
    B, D = c.shape
    n = w_ada.shape[1] // D
    return pl.pallas_call(
        _ada_kernel,
        grid=(n,),
        in_specs=[pl.BlockSpec((B, D), lambda j: (0, 0)),
                  pl.BlockSpec((D, D), lambda j: (0, j)),
                  pl.BlockSpec((1, D), lambda j: (0, j))],
        out_specs=pl.BlockSpec((B, D), lambda j: (0, j)),
        out_shape=jax.ShapeDtypeStruct((B, n * D), F32),
        name="ada",
    )(c, w_ada, b_ada.reshape(1, -1))


_C_Q = 0
_C_K = _C_Q + ATTN_WIDTH
_C_V = _C_K + KV_WIDTH
_C_QKM = _C_V + KV_WIDTH
_C_VM = _C_QKM + 2 * MLSTM_WIDTH
_C_OM = _C_VM + MLSTM_WIDTH
_C_G = _C_OM + MLSTM_WIDTH
_C_END = _C_G + LANES


def _inproj_kernel(x_ref, xp_ref, xn_ref, mod_ref, g_ref, w_ref, gb_ref, qg_ref, kg_ref, cos_ref, sa_ref, sb_ref,
                   bd_ref, cw_ref, cb_ref,
                   q_ref, k_ref, vt_ref, qm_ref, ktm_ref, vm_ref, om_ref, grow_ref, cpad_ref):
    D = D_MODEL
    mod = mod_ref[0]
    sh1 = mod[:, 0:D]
    sc1 = mod[:, D:2 * D]

    def normed(x):
        ms = jnp.mean(x * x, axis=-1, keepdims=True)
        h = (x * lax.rsqrt(ms + EPS)) * g_ref[...]
        return (h * (1.0 + sc1) + sh1).astype(BF16)

    hb = normed(x_ref[0])
    hb_halo = normed(jnp.concatenate([xp_ref[0], xn_ref[0]], axis=0))

    cosf = cos_ref[...]
    sa = sa_ref[...]
    sb = sb_ref[...]
    bd = bd_ref[...]

    def norm_rope(xs, gain, scale):
        ss = jnp.dot((xs * xs).astype(BF16), bd, preferred_element_type=F32)
        xn = (xs * lax.rsqrt(ss + EPS)) * gain
        out = xn * cosf + pltpu.roll(xn, LANES - 16, 1) * sa + pltpu.roll(xn, 16, 1) * sb
        return out * scale

    hd = ATTN_HEAD_DIM
    qkv = jnp.dot(hb, w_ref[:, _C_Q:_C_QKM], preferred_element_type=F32)
    for j in range(ATTN_WIDTH // LANES):
        qt = norm_rope(qkv[:, j * LANES:(j + 1) * LANES], qg_ref[...], Q_SCALE).T.astype(BF16)
        q_ref[0, 2 * j] = qt[:hd]
        q_ref[0, 2 * j + 1] = qt[hd:]
    ks = norm_rope(qkv[:, _C_K:_C_K + LANES], kg_ref[...], 1.0).astype(BF16)
    k_ref[0, 0] = ks[:, :hd]
    k_ref[0, 1] = ks[:, hd:]
    vt = qkv[:, _C_V:_C_V + LANES].T.astype(BF16)
    vt_ref[0, 0] = vt[:hd]
    vt_ref[0, 1] = vt[hd:]
    tm = hb.shape[0]
    halo = F32_SUBLANES
    qk_pre = jnp.dot(jnp.concatenate([hb, hb_halo], axis=0), w_ref[:, _C_QKM:_C_VM],
                     preferred_element_type=F32)
    i = pl.program_id(1)
    before = jnp.where(i > 0, qk_pre[tm:tm + halo], 0.0)
    after = jnp.where(i < pl.num_programs(1) - 1, qk_pre[tm + halo:], 0.0)
    rb = LANES
    for j0 in range(0, 2 * MLSTM_WIDTH, LANES):
        cols = slice(j0, j0 + LANES)
        slab = j0 // LANES
        cpad_ref[slab, 0:halo] = before[:, cols]
        cpad_ref[slab, halo:halo + tm] = qk_pre[0:tm, cols]
        cpad_ref[slab, halo + tm:2 * halo + tm] = after[:, cols]
        for r0 in range(0, tm, rb):
            acc = cb_ref[:, cols]
            for t in range(CONV_WIDTH):
                off = halo + r0 + t - CONV_WIDTH // 2
                acc = acc + cpad_ref[slab, off:off + rb, :] * cw_ref[t:t + 1, cols]
            y = acc * jax.nn.sigmoid(acc)
            if j0 < MLSTM_WIDTH:
                qm_ref[0, r0:r0 + rb, cols] = y.astype(BF16)
            else:
                ktm_ref[0, j0 - MLSTM_WIDTH:j0 - MLSTM_WIDTH + LANES, r0:r0 + rb] = \
                    (y * MLSTM_HEAD_DIM ** -0.5).T.astype(BF16)
    vm_ref[0] = jnp.dot(hb, w_ref[:, _C_VM:_C_OM], preferred_element_type=F32).astype(BF16)
    om_ref[0] = jnp.dot(hb, w_ref[:, _C_OM:_C_G], preferred_element_type=F32)
    gates = jnp.dot(hb, w_ref[:, _C_G:_C_END], preferred_element_type=F32) + gb_ref[...]
    gt = gates.T
    H = MLSTM_HEADS
    unused = jnp.zeros((GATE_SLOTS - 4, tm), F32)
    for h in range(H):
        grow_ref[0, h] = jnp.concatenate([gt[t * H + h:t * H + h + 1] for t in range(4)] + [unused], axis=0)


def _rope_tables(S):
    rows = S // GRID_W
    row = np.repeat(np.arange(rows, dtype=np.float64), GRID_W)
    col = np.tile(np.arange(GRID_W, dtype=np.float64), rows)
    axis_dim = ATTN_HEAD_DIM // 2
    inv_freq = (ROPE_THETA ** (-np.arange(0, axis_dim, 2, dtype=np.float32) / axis_dim)).astype(np.float32)
    ar = (row[:, None].astype(np.float32) * inv_freq[None, :]).astype(np.float64)
    ac = (col[:, None].astype(np.float32) * inv_freq[None, :]).astype(np.float64)
    cr, sr, cc, sc = np.cos(ar), np.sin(ar), np.cos(ac), np.sin(ac)
    z = np.zeros_like(sr)
    cos_h = np.concatenate([cr, cr, cc, cc], axis=-1)
    sa_h = np.concatenate([-sr, z, -sc, z], axis=-1)
    sb_h = np.concatenate([z, sr, z, sc], axis=-1)
    two = lambda a: jnp.asarray(np.concatenate([a, a], axis=-1), F32)
    return two(cos_h), two(sa_h), two(sb_h)


def _inproj(x, mod3, norm_g, w_in, gate_b, q_g, k_g, conv_w, conv_b):
    B, S, D = x.shape
    tm = TM_IN
    H = MLSTM_HEADS
    w = jnp.pad(w_in.astype(BF16), ((0, 0), (0, _C_END - w_in.shape[1])))
    gb = jnp.pad(gate_b.reshape(1, -1), ((0, 0), (0, LANES - N_GATES)))
    cosf, sa, sb = _rope_tables(S)
    qg2 = jnp.concatenate([q_g, q_g]).reshape(1, LANES)
    kg2 = jnp.concatenate([k_g, k_g]).reshape(1, LANES)
    blk = np.kron(np.eye(LANES // ATTN_HEAD_DIM), np.ones((ATTN_HEAD_DIM, ATTN_HEAD_DIM))) / ATTN_HEAD_DIM
    bd = jnp.asarray(blk, BF16)

    const = lambda shape: pl.BlockSpec(shape, lambda b, i: (0,) * len(shape))
    hd = ATTN_HEAD_DIM
    out_shape = (
        jax.ShapeDtypeStruct((B, ATTN_HEADS, hd, S), BF16),
        jax.ShapeDtypeStruct((B, ATTN_KV_HEADS, S, hd), BF16),
        jax.ShapeDtypeStruct((B, ATTN_KV_HEADS, hd, S), BF16),
        jax.ShapeDtypeStruct((B, S, MLSTM_WIDTH), BF16),
        jax.ShapeDtypeStruct((B, MLSTM_WIDTH, S), BF16),
        jax.ShapeDtypeStruct((B, S, MLSTM_WIDTH), BF16),
        jax.ShapeDtypeStruct((B, S, MLSTM_WIDTH), F32),
        jax.ShapeDtypeStruct((B, H, GATE_SLOTS, S), F32),
    )
    out_specs = (
        pl.BlockSpec((1, ATTN_HEADS, hd, tm), lambda b, i: (b, 0, 0, i)),
        pl.BlockSpec((1, ATTN_KV_HEADS, tm, hd), lambda b, i: (b, 0, i, 0)),
        pl.BlockSpec((1, ATTN_KV_HEADS, hd, tm), lambda b, i: (b, 0, 0, i)),
        pl.BlockSpec((1, tm, MLSTM_WIDTH), lambda b, i: (b, i, 0)),
        pl.BlockSpec((1, MLSTM_WIDTH, tm), lambda b, i: (b, 0, i)),
        pl.BlockSpec((1, tm, MLSTM_WIDTH), lambda b, i: (b, i, 0)),
        pl.BlockSpec((1, tm, MLSTM_WIDTH), lambda b, i: (b, i, 0)),
        pl.BlockSpec((1, H, GATE_SLOTS, tm), lambda b, i: (b, 0, 0, i)),
    )
    halo = F32_SUBLANES
    tiles = tm // halo
    in_specs = [
        pl.BlockSpec((1, tm, D), lambda b, i: (b, i, 0)),
        pl.BlockSpec((1, halo, D), lambda b, i: (b, jnp.maximum(i * tiles - 1, 0), 0)),
        pl.BlockSpec((1, halo, D), lambda b, i: (b, jnp.minimum((i + 1) * tiles, S // halo - 1), 0)),
        pl.BlockSpec((1, 1, 6 * D), lambda b, i: (b, 0, 0)),
        const((1, D)),
        const((D, _C_END)),
        const((1, LANES)),
        const((1, LANES)),
        const((1, LANES)),
        pl.BlockSpec((tm, LANES), lambda b, i: (i, 0)),
        pl.BlockSpec((tm, LANES), lambda b, i: (i, 0)),
        pl.BlockSpec((tm, LANES), lambda b, i: (i, 0)),
        const((LANES, LANES)),
        const((CONV_WIDTH, 2 * MLSTM_WIDTH)),
        const((1, 2 * MLSTM_WIDTH)),
    ]
    return pl.pallas_call(
        _inproj_kernel,
        grid=(B, S // tm),
        in_specs=in_specs,
        out_specs=out_specs,
        out_shape=out_shape,
        scratch_shapes=[
            pltpu.VMEM((2 * MLSTM_WIDTH // LANES, tm + 2 * halo, LANES), F32)],
        compiler_params=pltpu.CompilerParams(
            dimension_semantics=("parallel", "parallel"), vmem_limit_bytes=VMEM_LIMIT),
        name="inproj",
    )(x, x, x, mod3, norm_g.reshape(1, D), w, gb, qg2, kg2, cosf, sa, sb, bd, conv_w, conv_b.reshape(1, -1))


def _attn_kernel(bound_ref, q_ref, k_ref, vt_ref, o_ref, s0_ref, vx_ref):
    G = ATTN_GROUP
    hd = ATTN_HEAD_DIM
    S = k_ref.shape[2]
    R = G * TQ
    nk = S // TK
    nq = S // TQ

    def load_q(qi):
        r0 = pl.multiple_of(qi * TQ, TQ)
        return jnp.concatenate([q_ref[0, g, :, pl.ds(r0, TQ)] for g in range(G)], axis=1)

    def scores(qb, kj, tk=TK):
        kb = k_ref[0, 0, kj * tk:(kj + 1) * tk, :]
        return jnp.dot(kb, qb, preferred_element_type=F32)

    def update(kj, s, m, acc):
        mn = jnp.maximum(m, jnp.max(s, axis=0, keepdims=True))
        alpha = jnp.exp2(m - mn)
        p = jnp.exp2(s - mn).astype(BF16)
        vb = vx_ref[:, kj * TK:(kj + 1) * TK]
        acc = alpha * acc + jnp.dot(vb, p, preferred_element_type=F32)
        return mn, acc

    def finish(qi, num, den):
        o = (num / den).astype(BF16)
        r0 = pl.multiple_of(qi * TQ, TQ)
        for g in range(G):
            o_ref[0, g * hd:(g + 1) * hd, pl.ds(r0, TQ)] = o[:, g * TQ:(g + 1) * TQ]

    bound = bound_ref[0]

    @pl.when(bound <= SAFE_SCORE_BOUND)
    def _():
        def qblock(qi, carry):
            qb = load_q(qi)
            acc = jnp.zeros((hd, R), F32)
            l = jnp.zeros((F32_SUBLANES, R), F32)
            tk = TK_BOUNDED
            ahead = [scores(qb, kj, tk) for kj in range(SCORE_TILES_AHEAD)]
            for kj in range(S // tk):
                if kj + SCORE_TILES_AHEAD < S // tk:
                    ahead.append(scores(qb, kj + SCORE_TILES_AHEAD, tk))
                p = jnp.exp2(ahead[kj] - bound)
                l = l + jnp.sum(p.reshape(tk // F32_SUBLANES, F32_SUBLANES, R), axis=0)
                acc = acc + jnp.dot(vt_ref[0, 0, :, kj * tk:(kj + 1) * tk], p.astype(BF16),
                                    preferred_element_type=F32)
            finish(qi, acc, jnp.sum(l, axis=0, keepdims=True))
            return carry

        lax.fori_loop(0, nq, qblock, 0, unroll=2)

    @pl.when(bound > SAFE_SCORE_BOUND)
    def _():
        vx_ref[0:hd] = vt_ref[0, 0]
        extra = lax.broadcasted_iota(jnp.int32, (VX_ROWS - hd, S), 0)
        vx_ref[hd:VX_ROWS] = (extra == 0).astype(BF16)
        s0_ref[...] = scores(load_q(0), 0)

        def qblock(qi, carry):
            qb = load_q(qi)
            ss = [scores(qb, kj) for kj in range(1, nk)]
            s_next = scores(load_q(jnp.minimum(qi + 1, nq - 1)), 0)
            m = jnp.full((1, R), -jnp.inf, F32)
            acc = jnp.zeros((VX_ROWS, R), F32)
            m, acc = update(0, s0_ref[...], m, acc)
            for kj in range(1, nk):
                m, acc = update(kj, ss[kj - 1], m, acc)
            s0_ref[...] = s_next
            finish(qi, acc[0:hd], acc[hd:hd + 1])
            return carry

        lax.fori_loop(0, nq, qblock, 0)


def _attention(q, k, vt, q_g, k_g):
    B, _, hd, S = q.shape
    G = ATTN_GROUP
    bound = (SCORE_BOUND_MARGIN * hd * Q_SCALE * jnp.max(jnp.abs(q_g)) * jnp.max(jnp.abs(k_g))).reshape(1)
    return pl.pallas_call(
        _attn_kernel,
        grid=(B, ATTN_KV_HEADS),
        in_specs=[
            pl.BlockSpec(memory_space=pltpu.SMEM),
            pl.BlockSpec((1, G, hd, S), lambda b, kv: (b, kv, 0, 0)),
            pl.BlockSpec((1, 1, S, hd), lambda b, kv: (b, kv, 0, 0)),
            pl.BlockSpec((1, 1, hd, S), lambda b, kv: (b, kv, 0, 0)),
        ],
        out_specs=pl.BlockSpec((1, G * hd, S), lambda b, kv: (b, kv, 0)),
        out_shape=jax.ShapeDtypeStruct((B, ATTN_WIDTH, S), BF16),
        scratch_shapes=[pltpu.VMEM((TK, G * TQ), F32),
                        pltpu.VMEM((VX_ROWS, S), BF16)],
        compiler_params=pltpu.CompilerParams(
            dimension_semantics=("parallel", "parallel"), vmem_limit_bytes=VMEM_LIMIT),
        name="attn",
    )(bound.astype(F32), q, k, vt)


def _log_sigmoid(x):
    return jnp.minimum(x, 0.0) - jnp.log(1.0 + jnp.exp(-jnp.abs(x)))


def _mlstm_kernel(qs_ref, kt_ref, v_ref, o_ref, grow_ref, ng_ref, y_ref,
                  rows_ref, col_ref, hf_ref, hb_ref, ct_ref):
    S = v_ref.shape[1]
    L = CHUNK
    assert L == LANES, "chunk tiles are square (L, 128) tiles"
    nc = S // L
    Dh = MLSTM_HEAD_DIM
    RB = 512

    ri = lax.broadcasted_iota(jnp.int32, (L, L), 0)
    ci = lax.broadcasted_iota(jnp.int32, (L, L), 1)
    lower = ci <= ri
    upper = ri <= ci
    tril_b = lower.astype(BF16)
    triu_b = upper.astype(BF16)

    def split3(x):
        t1 = x.astype(BF16).astype(F32)
        e1 = x - t1
        t2 = e1.astype(BF16).astype(F32)
        t3 = (e1 - t2).astype(BF16).astype(F32)
        return t1, t2, t3

    NR = GATE_SLOTS * nc
    heads = qs_ref.shape[2] // Dh
    tri2 = jnp.concatenate([triu_b, tril_b], axis=1)
    rj = lax.broadcasted_iota(jnp.int32, (NR, L), 0) & (GATE_SLOTS - 1)
    lane = lax.broadcasted_iota(jnp.int32, (NR, L), 1)
    is_fw = rj == 0
    is_r = (rj & 5) == 0
    src_rows = (1, 0, 4, 3, 2, 6)
    kk = lax.broadcasted_iota(jnp.int32, (4 * GATE_SLOTS, LANES), 0)
    blk = lax.broadcasted_iota(jnp.int32, (4 * GATE_SLOTS, LANES), 1)
    want = jnp.full((4 * GATE_SLOTS, LANES), -1, jnp.int32)
    for q, j in enumerate(src_rows):
        want = jnp.where(blk == q, j, want)
    sel = ((kk & (GATE_SLOTS - 1)) == jnp.where(kk < 3 * GATE_SLOTS, want, -1)).astype(BF16)
    zeros8 = jnp.zeros((GATE_SLOTS, L), F32)
    for hh in range(heads):
        graw = jnp.concatenate([grow_ref[0, hh, :, c * L:(c + 1) * L] for c in range(nc)], axis=0)
        ps = sum(jnp.dot(t.astype(BF16), tri2, preferred_element_type=F32)
                 for t in split3(_log_sigmoid(graw) * LOG2E))
        pre, suf = ps[:, :L], ps[:, L:]
        rr = graw * LOG2E - jnp.where(is_fw, pltpu.roll(pre, NR - 1, 0), pltpu.roll(suf, NR - 1, 0))
        x = jnp.where(is_r, rr, -jnp.inf)
        sh = 1
        while sh < L:
            fwd = jnp.where(lane >= sh, pltpu.roll(x, sh, 1), -jnp.inf)
            bwd = jnp.where(lane < L - sh, pltpu.roll(x, L - sh, 1), -jnp.inf)
            x = jnp.maximum(x, jnp.where(is_fw, fwd, bwd))
            sh *= 2
        quant = jnp.where(is_r, rr, jnp.where(rj == 1, pre, jnp.where(rj == 3, suf, jnp.where(
            (rj & 5) == 4, pltpu.roll(x, 4, 0), 0.0))))
        terms = split3(quant)
        stacked = []
        for c in range(nc):
            rows = slice(c * GATE_SLOTS, (c + 1) * GATE_SLOTS)
            rows_ref[hh, :, c * L:(c + 1) * L] = quant[rows]
            stacked.append(jnp.concatenate([t[rows] for t in terms] + [zeros8], axis=0))
        col_ref[hh] = _tn_dot(jnp.concatenate(stacked, axis=1).astype(BF16), sel)

    ones_blk = jnp.ones((L, LANES), BF16)
    ct_ref[...] = jnp.zeros_like(ct_ref)

    def direction(c, m, hh, d_idx, mask, g_row, out_ref):
        c0 = pl.multiple_of(c * L, L)
        hcols = slice(hh * Dh, (hh + 1) * Dh)
        qc = qs_ref[0, pl.ds(c0, L), hcols]
        ktc = kt_ref[0, hcols, pl.ds(c0, L)]
        vc = v_ref[0, pl.ds(c0, L), hcols]
        r_row = rows_ref[hh, 2 * d_idx:2 * d_idx + 1, pl.ds(c0, L)]
        colblk = col_ref[hh, pl.ds(c0, L), :]
        b = jnp.broadcast_to(colblk[:, 3 * d_idx:3 * d_idx + 1], (L, LANES))
        r = jnp.broadcast_to(colblk[:, 3 * d_idx + 1:3 * d_idx + 2], (L, LANES))
        cmax = jnp.broadcast_to(colblk[:, 3 * d_idx + 2:3 * d_idx + 3], (L, LANES))
        rmax = cmax[g_row:g_row + 1]
        p = jnp.exp2(jnp.where(mask, r_row - cmax, -jnp.inf))
        s = (jnp.dot(qc, ktc, preferred_element_type=F32) * p).astype(BF16)
        intra = jnp.dot(s, jnp.concatenate([vc, ones_blk], axis=1), preferred_element_type=F32)
        w = jnp.exp2(r - rmax)
        wv = jnp.concatenate([(w * vc.astype(F32)).astype(BF16), w.astype(BF16)], axis=1)
        upd = jnp.dot(ktc, wv, preferred_element_type=F32)
        mc = jnp.maximum(m, cmax)
        w_inter = jnp.exp2(m - mc)
        w_intra = jnp.exp2(cmax - mc)
        ct = ct_ref[2 * hh + d_idx]
        inter = jnp.dot(qc, ct.astype(BF16), preferred_element_type=F32)
        num = w_inter * inter[:, :Dh] + w_intra * intra[:, :Dh]
        den = w_inter * inter[:, Dh:] + w_intra * intra[:, Dh:]
        out_ref[hh, pl.ds(c0, L), :] = num / jnp.maximum(jnp.abs(den), jnp.exp2(-(b + mc)))
        g = b[g_row:g_row + 1]
        mu = jnp.maximum(m, rmax)
        decay = jnp.exp2(m - mu)
        gain = jnp.exp2(rmax - mu)
        ct_ref[2 * hh + d_idx] = jnp.concatenate([decay, decay], axis=1) * ct \
            + jnp.concatenate([gain, gain], axis=1) * upd
        return g + mu

    def step(j, carry):
        out = []
        for hh in range(heads):
            out.append(direction(j, carry[2 * hh], hh, 0, lower, L - 1, hf_ref))
            out.append(direction(nc - 1 - j, carry[2 * hh + 1], hh, 1, upper, 0, hb_ref))
        return tuple(out)

    zero_row = jnp.zeros((1, LANES), F32)
    lax.fori_loop(0, nc, step, (zero_row,) * (2 * heads), unroll=MLSTM_LOOP_UNROLL)

    for hh in range(heads):
        hcols = slice(hh * Dh, (hh + 1) * Dh)
        ng = ng_ref[:, hcols]
        for r in range(S // RB):
            sl = slice(r * RB, (r + 1) * RB)
            hsum = hf_ref[hh, sl] + hb_ref[hh, sl]
            hn = hsum * lax.rsqrt(jnp.mean(hsum * hsum, axis=-1, keepdims=True) + EPS)
            hn = hn * ng
            y_ref[0, sl, hcols] = (hn * jax.nn.sigmoid(o_ref[0, sl, hcols])).astype(BF16)


def _mlstm(qm, ktm, vm, om, grow, norm_g):
    B, S, _ = vm.shape
    H = MLSTM_HEADS
    Dh = MLSTM_HEAD_DIM
    hps = MLSTM_HEADS_PER_STEP
    W = hps * Dh
    seq = pl.BlockSpec((1, S, W), lambda b, h: (b, 0, h))
    return pl.pallas_call(
        _mlstm_kernel,
        grid=(B, H // hps),
        in_specs=[
            seq,
            pl.BlockSpec((1, W, S), lambda b, h: (b, h, 0)),
            seq, seq,
            pl.BlockSpec((1, hps, GATE_SLOTS, S), lambda b, h: (b, h, 0, 0)),
            pl.BlockSpec((1, W), lambda b, h: (0, h)),
        ],
        out_specs=pl.BlockSpec((1, S, W), lambda b, h: (b, 0, h)),
        out_shape=jax.ShapeDtypeStruct((B, S, MLSTM_WIDTH), BF16),
        scratch_shapes=[
            pltpu.VMEM((hps, GATE_SLOTS, S), F32),
            pltpu.VMEM((hps, S, LANES), F32),
            pltpu.VMEM((hps, S, Dh), F32),
            pltpu.VMEM((hps, S, Dh), F32),
            pltpu.VMEM((2 * hps, Dh, 2 * Dh), F32),
        ],
        compiler_params=pltpu.CompilerParams(
            dimension_semantics=("parallel", "parallel"), vmem_limit_bytes=VMEM_LIMIT),
        name="mlstm",
    )(qm, ktm, vm, om, grow, norm_g.reshape(1, -1))


def _outffn_kernel(x_ref, mod_ref, yat_ref, ym_ref, wo_ref, gf_ref, wg_ref, wu_ref, wd_ref, fg_ref, o_ref):
    D = D_MODEL
    x = x_ref[0]
    mod = mod_ref[0]
    g1 = mod[:, 2 * D:3 * D]
    sh2 = mod[:, 3 * D:4 * D]
    sc2 = mod[:, 4 * D:5 * D]
    g2 = mod[:, 5 * D:6 * D]
    tm = x.shape[0]
    halves = [slice(0, tm // 2), slice(tm // 2, tm)]
    x1s = []
    for rows in halves:
        mix = _tn_dot(yat_ref[0, :, rows], wo_ref[0:ATTN_WIDTH, :]) \
            + jnp.dot(ym_ref[0, rows, :], wo_ref[ATTN_WIDTH:MIX_WIDTH, :], preferred_element_type=F32)
        x1s.append(x[rows] + g1 * mix)
    for rows, x1 in zip(halves, x1s):
        ms = jnp.mean(x1 * x1, axis=-1, keepdims=True)
        h = (x1 * lax.rsqrt(ms + EPS)) * gf_ref[...]
        hb = (h * (1.0 + sc2) + sh2).astype(BF16)
        gate = jnp.dot(hb, wg_ref[...], preferred_element_type=F32)
        up = jnp.dot(hb, wu_ref[...], preferred_element_type=F32)
        act = (gate * jax.nn.sigmoid(gate) * up).astype(BF16)
        ffn = jnp.dot(act, wd_ref[...], preferred_element_type=F32)
        x2 = x1 + g2 * ffn
        ms2 = jnp.mean(x2 * x2, axis=-1, keepdims=True)
        o_ref[0, rows, :] = (x2 * lax.rsqrt(ms2 + EPS)) * fg_ref[...]


def _outffn(x, mod3, yat, ym, w_out, norm_ffn_g, w_gate, w_up, w_down, final_g):
    B, S, D = x.shape
    tm = TM_FFN
    F = w_gate.shape[1]
    const = lambda shape: pl.BlockSpec(shape, lambda b, i: (0,) * len(shape), pipeline_mode=pl.Buffered(1))
    return pl.pallas_call(
        _outffn_kernel,
        grid=(B, S // tm),
        in_specs=[
            pl.BlockSpec((1, tm, D), lambda b, i: (b, i, 0)),
            pl.BlockSpec((1, 1, 6 * D), lambda b, i: (b, 0, 0)),
            pl.BlockSpec((1, ATTN_WIDTH, tm), lambda b, i: (b, 0, i)),
            pl.BlockSpec((1, tm, MLSTM_WIDTH), lambda b, i: (b, i, 0)),
            const((MIX_WIDTH, D)),
            const((1, D)),
            const((D, F)),
            const((D, F)),
            const((F, D)),
            const((1, D)),
        ],
        out_specs=pl.BlockSpec((1, tm, D), lambda b, i: (b, i, 0)),
        out_shape=jax.ShapeDtypeStruct((B, S, D), F32),
        compiler_params=pltpu.CompilerParams(
            dimension_semantics=("parallel", "parallel"), vmem_limit_bytes=VMEM_LIMIT),
        name="outffn",
    )(x, mod3, yat, ym, w_out.astype(BF16), norm_ffn_g.reshape(1, D), w_gate.astype(BF16),
      w_up.astype(BF16), w_down.astype(BF16), final_g.reshape(1, D))


def kernel(x, c, w_ada, b_ada, norm_mix_g, w_in, q_norm_g, k_norm_g, conv_w, conv_b, gate_b,
           mlstm_norm_g, w_out, norm_ffn_g, w_gate, w_up, w_down, final_norm_g):
    assert w_ada.shape[0] == 1, "the final RMSNorm is fused into the single layer's last kernel"
    B = x.shape[0]
    l = 0
    mod3 = _ada(c, w_ada[l], b_ada[l]).reshape(B, 1, -1)
    q, k, vt, qm, ktm, vm, om, grow = _inproj(x, mod3, norm_mix_g[l], w_in[l], gate_b[l],
                                              q_norm_g[l], k_norm_g[l], conv_w[l], conv_b[l])
    yat = _attention(q, k, vt, q_norm_g[l], k_norm_g[l])
    ym = _mlstm(qm, ktm, vm, om, grow, mlstm_norm_g[l])
    return _outffn(x, mod3, yat, ym, w_out[l], norm_ffn_g[l], w_gate[l], w_up[l], w_down[l], final_norm_g)
```

```python
import jax
import jax.numpy as jnp
import numpy as np
from jax import lax
from jax.experimental import pallas as pl
from jax.experimental.pallas import tpu as pltpu

F32 = jnp.float32
BF16 = jnp.bfloat16

D_MODEL = 1024
GRID_W = 64
EPS = 1e-6
ATTN_HEADS = 8
ATTN_KV_HEADS = 2
ATTN_GROUP = ATTN_HEADS // ATTN_KV_HEADS
ATTN_HEAD_DIM = 64
ATTN_WIDTH = ATTN_HEADS * ATTN_HEAD_DIM
KV_WIDTH = ATTN_KV_HEADS * ATTN_HEAD_DIM
ROPE_THETA = 10000.0
MLSTM_HEADS = 4
MLSTM_HEAD_DIM = 128
MLSTM_WIDTH = MLSTM_HEADS * MLSTM_HEAD_DIM
CONV_WIDTH = 5
N_GATES = 4 * MLSTM_HEADS
MIX_WIDTH = ATTN_WIDTH + MLSTM_WIDTH

LANES = 128
GATE_SLOTS = 8

TM_IN = 1024
TM_FFN = 512
TQ = 128
TK = 256
F32_SUBLANES = 8
BF16_SUBLANES = 16
LOG2E = float(np.log2(np.e))
VX_ROWS = ATTN_HEAD_DIM + BF16_SUBLANES
CHUNK = 128
MLSTM_HEADS_PER_STEP = 1
MLSTM_LOOP_UNROLL = 8
Q_SCALE = ATTN_HEAD_DIM ** -0.5 * float(np.log2(np.e))
VMEM_LIMIT = 56 * 1024 * 1024
SAFE_SCORE_BOUND = 50.0
TK_BOUNDED = 256
SCORE_TILES_AHEAD = 2
QBLOCKS_PER_BODY = 4
SCORE_BOUND_MARGIN = 1.02


def _tn_dot(a, b):
    return lax.dot_general(a, b, (((0,), (0,)), ((), ())), preferred_element_type=F32)


def _ada_kernel(c_ref, w_ref, b_ref, o_ref):
    c = c_ref[...]
    cond = c * jax.nn.sigmoid(c)
    o_ref[...] = jnp.dot(cond, w_ref[...], preferred_element_type=F32,
                         precision=lax.Precision.HIGHEST) + b_ref[...]


def _ada(c, w_ada, b_ada):
    B, D = c.shape
    n = w_ada.shape[1] // D
    return pl.pallas_call(
        _ada_kernel,
        grid=(n,),
        in_specs=[pl.BlockSpec((B, D), lambda j: (0, 0)),
                  pl.BlockSpec((D, D), lambda j: (0, j)),
                  pl.BlockSpec((1, D), lambda j: (0, j))],
        out_specs=pl.BlockSpec((B, D), lambda j: (0, j)),
        out_shape=jax.ShapeDtypeStruct((B, n * D), F32),
        name="ada",
    )(c, w_ada, b_ada.reshape(1, -1))


_C_Q = 0
_C_K = _C_Q + ATTN_WIDTH
_C_V = _C_K + KV_WIDTH
_C_QKM = _C_V + KV_WIDTH
_C_VM = _C_QKM + 2 * MLSTM_WIDTH
_C_OM = _C_VM + MLSTM_WIDTH
_C_G = _C_OM + MLSTM_WIDTH
_C_END = _C_G + LANES


def _inproj_kernel(x_ref, xp_ref, xn_ref, mod_ref, g_ref, w_ref, gb_ref, qg_ref, kg_ref, cos_ref, sa_ref, sb_ref,
                   bd_ref, cw_ref, cb_ref,
                   q_ref, k_ref, vt_ref, qm_ref, ktm_ref, vm_ref, om_ref, grow_ref, cpad_ref):
    D = D_MODEL
    mod = mod_ref[0]
    sh1 = mod[:, 0:D]
    sc1 = mod[:, D:2 * D]

    def normed(x):
        ms = jnp.mean(x * x, axis=-1, keepdims=True)
        h = (x * lax.rsqrt(ms + EPS)) * g_ref[...]
        return (h * (1.0 + sc1) + sh1).astype(BF16)

    hb = normed(x_ref[0])
    hb_halo = normed(jnp.concatenate([xp_ref[0], xn_ref[0]], axis=0))

    cosf = cos_ref[...]
    sa = sa_ref[...]
    sb = sb_ref[...]
    bd = bd_ref[...]

    def norm_rope(xs, gain, scale):
        ss = jnp.dot((xs * xs).astype(BF16), bd, preferred_element_type=F32)
        xn = (xs * lax.rsqrt(ss + EPS)) * gain
        out = xn * cosf + pltpu.roll(xn, LANES - 16, 1) * sa + pltpu.roll(xn, 16, 1) * sb
        return out * scale

    hd = ATTN_HEAD_DIM
    qkv = jnp.dot(hb, w_ref[:, _C_Q:_C_QKM], preferred_element_type=F32)
    for j in range(ATTN_WIDTH // LANES):
        qt = norm_rope(qkv[:, j * LANES:(j + 1) * LANES], qg_ref[...], Q_SCALE).T.astype(BF16)
        q_ref[0, 2 * j] = qt[:hd]
        q_ref[0, 2 * j + 1] = qt[hd:]
    ks = norm_rope(qkv[:, _C_K:_C_K + LANES], kg_ref[...], 1.0).astype(BF16)
    k_ref[0, 0] = ks[:, :hd]
    k_ref[0, 1] = ks[:, hd:]
    vt = qkv[:, _C_V:_C_V + LANES].T.astype(BF16)
    vt_ref[0, 0] = vt[:hd]
    vt_ref[0, 1] = vt[hd:]
    tm = hb.shape[0]
    halo = F32_SUBLANES
    qk_pre = jnp.dot(jnp.concatenate([hb, hb_halo], axis=0), w_ref[:, _C_QKM:_C_VM],
                     preferred_element_type=F32)
    i = pl.program_id(1)
    before = jnp.where(i > 0, qk_pre[tm:tm + halo], 0.0)
    after = jnp.where(i < pl.num_programs(1) - 1, qk_pre[tm + halo:], 0.0)
    rb = LANES
    for j0 in range(0, 2 * MLSTM_WIDTH, LANES):
        cols = slice(j0, j0 + LANES)
        slab = j0 // LANES
        cpad_ref[slab, 0:halo] = before[:, cols]
        cpad_ref[slab, halo:halo + tm] = qk_pre[0:tm, cols]
        cpad_ref[slab, halo + tm:2 * halo + tm] = after[:, cols]
        for r0 in range(0, tm, rb):
            acc = cb_ref[:, cols]
            for t in range(CONV_WIDTH):
                off = halo + r0 + t - CONV_WIDTH // 2
                acc = acc + cpad_ref[slab, off:off + rb, :] * cw_ref[t:t + 1, cols]
            y = acc * jax.nn.sigmoid(acc)
            if j0 < MLSTM_WIDTH:
                qm_ref[0, r0:r0 + rb, cols] = y.astype(BF16)
            else:
                ktm_ref[0, j0 - MLSTM_WIDTH:j0 - MLSTM_WIDTH + LANES, r0:r0 + rb] = \
                    (y * MLSTM_HEAD_DIM ** -0.5).T.astype(BF16)
    vm_ref[0] = jnp.dot(hb, w_ref[:, _C_VM:_C_OM], preferred_element_type=F32).astype(BF16)
    om_ref[0] = jnp.dot(hb, w_ref[:, _C_OM:_C_G], preferred_element_type=F32)
    gates = jnp.dot(hb, w_ref[:, _C_G:_C_END], preferred_element_type=F32) + gb_ref[...]
    gt = gates.T
    H = MLSTM_HEADS
    unused = jnp.zeros((GATE_SLOTS - 4, tm), F32)
    for h in range(H):
        grow_ref[0, h] = jnp.concatenate([gt[t * H + h:t * H + h + 1] for t in range(4)] + [unused], axis=0)


def _rope_tables(S):
    rows = S // GRID_W
    row = np.repeat(np.arange(rows, dtype=np.float64), GRID_W)
    col = np.tile(np.arange(GRID_W, dtype=np.float64), rows)
    axis_dim = ATTN_HEAD_DIM // 2
    inv_freq = (ROPE_THETA ** (-np.arange(0, axis_dim, 2, dtype=np.float32) / axis_dim)).astype(np.float32)
    ar = (row[:, None].astype(np.float32) * inv_freq[None, :]).astype(np.float64)
    ac = (col[:, None].astype(np.float32) * inv_freq[None, :]).astype(np.float64)
    cr, sr, cc, sc = np.cos(ar), np.sin(ar), np.cos(ac), np.sin(ac)
    z = np.zeros_like(sr)
    cos_h = np.concatenate([cr, cr, cc, cc], axis=-1)
    sa_h = np.concatenate([-sr, z, -sc, z], axis=-1)
    sb_h = np.concatenate([z, sr, z, sc], axis=-1)
    two = lambda a: jnp.asarray(np.concatenate([a, a], axis=-1), F32)
    return two(cos_h), two(sa_h), two(sb_h)


def _inproj(x, mod3, norm_g, w_in, gate_b, q_g, k_g, conv_w, conv_b):
    B, S, D = x.shape
    tm = TM_IN
    H = MLSTM_HEADS
    w = jnp.pad(w_in.astype(BF16), ((0, 0), (0, _C_END - w_in.shape[1])))
    gb = jnp.pad(gate_b.reshape(1, -1), ((0, 0), (0, LANES - N_GATES)))
    cosf, sa, sb = _rope_tables(S)
    qg2 = jnp.concatenate([q_g, q_g]).reshape(1, LANES)
    kg2 = jnp.concatenate([k_g, k_g]).reshape(1, LANES)
    blk = np.kron(np.eye(LANES // ATTN_HEAD_DIM), np.ones((ATTN_HEAD_DIM, ATTN_HEAD_DIM))) / ATTN_HEAD_DIM
    bd = jnp.asarray(blk, BF16)

    const = lambda shape: pl.BlockSpec(shape, lambda b, i: (0,) * len(shape))
    hd = ATTN_HEAD_DIM
    out_shape = (
        jax.ShapeDtypeStruct((B, ATTN_HEADS, hd, S), BF16),
        jax.ShapeDtypeStruct((B, ATTN_KV_HEADS, S, hd), BF16),
        jax.ShapeDtypeStruct((B, ATTN_KV_HEADS, hd, S), BF16),
        jax.ShapeDtypeStruct((B, S, MLSTM_WIDTH), BF16),
        jax.ShapeDtypeStruct((B, MLSTM_WIDTH, S), BF16),
        jax.ShapeDtypeStruct((B, S, MLSTM_WIDTH), BF16),
        jax.ShapeDtypeStruct((B, S, MLSTM_WIDTH), F32),
        jax.ShapeDtypeStruct((B, H, GATE_SLOTS, S), F32),
    )
    out_specs = (
        pl.BlockSpec((1, ATTN_HEADS, hd, tm), lambda b, i: (b, 0, 0, i)),
        pl.BlockSpec((1, ATTN_KV_HEADS, tm, hd), lambda b, i: (b, 0, i, 0)),
        pl.BlockSpec((1, ATTN_KV_HEADS, hd, tm), lambda b, i: (b, 0, 0, i)),
        pl.BlockSpec((1, tm, MLSTM_WIDTH), lambda b, i: (b, i, 0)),
        pl.BlockSpec((1, MLSTM_WIDTH, tm), lambda b, i: (b, 0, i)),
        pl.BlockSpec((1, tm, MLSTM_WIDTH), lambda b, i: (b, i, 0)),
        pl.BlockSpec((1, tm, MLSTM_WIDTH), lambda b, i: (b, i, 0)),
        pl.BlockSpec((1, H, GATE_SLOTS, tm), lambda b, i: (b, 0, 0, i)),
    )
    halo = F32_SUBLANES
    tiles = tm // halo
    in_specs = [
        pl.BlockSpec((1, tm, D), lambda b, i: (b, i, 0)),
        pl.BlockSpec((1, halo, D), lambda b, i: (b, jnp.maximum(i * tiles - 1, 0), 0)),
        pl.BlockSpec((1, halo, D), lambda b, i: (b, jnp.minimum((i + 1) * tiles, S // halo - 1), 0)),
        pl.BlockSpec((1, 1, 6 * D), lambda b, i: (b, 0, 0)),
        const((1, D)),
        const((D, _C_END)),
        const((1, LANES)),
        const((1, LANES)),
        const((1, LANES)),
        pl.BlockSpec((tm, LANES), lambda b, i: (i, 0)),
        pl.BlockSpec((tm, LANES), lambda b, i: (i, 0)),
        pl.BlockSpec((tm, LANES), lambda b, i: (i, 0)),
        const((LANES, LANES)),
        const((CONV_WIDTH, 2 * MLSTM_WIDTH)),
        const((1, 2 * MLSTM_WIDTH)),
    ]
    return pl.pallas_call(
        _inproj_kernel,
        grid=(B, S // tm),
        in_specs=in_specs,
        out_specs=out_specs,
        out_shape=out_shape,
        scratch_shapes=[
            pltpu.VMEM((2 * MLSTM_WIDTH // LANES, tm + 2 * halo, LANES), F32)],
        compiler_params=pltpu.CompilerParams(
            dimension_semantics=("parallel", "parallel"), vmem_limit_bytes=VMEM_LIMIT),
        name="inproj",
    )(x, x, x, mod3, norm_g.reshape(1, D), w, gb, qg2, kg2, cosf, sa, sb, bd, conv_w, conv_b.reshape(1, -1))


def _attn_kernel(bound_ref, q_ref, k_ref, vt_ref, o_ref, s0_ref, vx_ref):
    G = ATTN_GROUP
    hd = ATTN_HEAD_DIM
    S = k_ref.shape[2]
    R = G * TQ
    nk = S // TK
    nq = S // TQ

    def load_q(qi):
        r0 = pl.multiple_of(qi * TQ, TQ)
        return jnp.concatenate([q_ref[0, g, :, pl.ds(r0, TQ)] for g in range(G)], axis=1)

    def scores(qb, kj, tk=TK):
        kb = k_ref[0, 0, kj * tk:(kj + 1) * tk, :]
        return jnp.dot(kb, qb, preferred_element_type=F32)

    def update(kj, s, m, acc):
        mn = jnp.maximum(m, jnp.max(s, axis=0, keepdims=True))
        alpha = jnp.exp2(m - mn)
        p = jnp.exp2(s - mn).astype(BF16)
        vb = vx_ref[:, kj * TK:(kj + 1) * TK]
        acc = alpha * acc + jnp.dot(vb, p, preferred_element_type=F32)
        return mn, acc

    def finish(qi, num, den):
        o = (num / den).astype(BF16)
        r0 = pl.multiple_of(qi * TQ, TQ)
        for g in range(G):
            o_ref[0, g * hd:(g + 1) * hd, pl.ds(r0, TQ)] = o[:, g * TQ:(g + 1) * TQ]

    bound = bound_ref[0]

    @pl.when(bound <= SAFE_SCORE_BOUND)
    def _():
        tk = TK_BOUNDED
        group = QBLOCKS_PER_BODY

        def qgroup(gi, carry):
            qbs = [load_q(gi * group + b) for b in range(group)]
            stream = [(b, kj) for b in range(group) for kj in range(S // tk)]
            tiles = {i: scores(qbs[b], kj, tk) for i, (b, kj) in enumerate(stream[:SCORE_TILES_AHEAD])}
            acc = l = None
            for i, (b, kj) in enumerate(stream):
                nxt = i + SCORE_TILES_AHEAD
                if nxt < len(stream):
                    tiles[nxt] = scores(qbs[stream[nxt][0]], stream[nxt][1], tk)
                if kj == 0:
                    acc = jnp.zeros((hd, R), F32)
                    l = jnp.zeros((F32_SUBLANES, R), F32)
                p = jnp.exp2(tiles.pop(i) - bound)
                l = l + jnp.sum(p.reshape(tk // F32_SUBLANES, F32_SUBLANES, R), axis=0)
                acc = acc + jnp.dot(vt_ref[0, 0, :, kj * tk:(kj + 1) * tk], p.astype(BF16),
                                    preferred_element_type=F32)
                if kj == S // tk - 1:
                    finish(gi * group + b, acc, jnp.sum(l, axis=0, keepdims=True))
            return carry

        lax.fori_loop(0, nq // group, qgroup, 0)

    @pl.when(bound > SAFE_SCORE_BOUND)
    def _():
        vx_ref[0:hd] = vt_ref[0, 0]
        extra = lax.broadcasted_iota(jnp.int32, (VX_ROWS - hd, S), 0)
        vx_ref[hd:VX_ROWS] = (extra == 0).astype(BF16)
        s0_ref[...] = scores(load_q(0), 0)

        def qblock(qi, carry):
            qb = load_q(qi)
            ss = [scores(qb, kj) for kj in range(1, nk)]
            s_next = scores(load_q(jnp.minimum(qi + 1, nq - 1)), 0)
            m = jnp.full((1, R), -jnp.inf, F32)
            acc = jnp.zeros((VX_ROWS, R), F32)
            m, acc = update(0, s0_ref[...], m, acc)
            for kj in range(1, nk):
                m, acc = update(kj, ss[kj - 1], m, acc)
            s0_ref[...] = s_next
            finish(qi, acc[0:hd], acc[hd:hd + 1])
            return carry

        lax.fori_loop(0, nq, qblock, 0)


def _attention(q, k, vt, q_g, k_g):
    B, _, hd, S = q.shape
    G = ATTN_GROUP
    bound = (SCORE_BOUND_MARGIN * hd * Q_SCALE * jnp.max(jnp.abs(q_g)) * jnp.max(jnp.abs(k_g))).reshape(1)
    return pl.pallas_call(
        _attn_kernel,
        grid=(B, ATTN_KV_HEADS),
        in_specs=[
            pl.BlockSpec(memory_space=pltpu.SMEM),
            pl.BlockSpec((1, G, hd, S), lambda b, kv: (b, kv, 0, 0)),
            pl.BlockSpec((1, 1, S, hd), lambda b, kv: (b, kv, 0, 0)),
            pl.BlockSpec((1, 1, hd, S), lambda b, kv: (b, kv, 0, 0)),
        ],
        out_specs=pl.BlockSpec((1, G * hd, S), lambda b, kv: (b, kv, 0)),
        out_shape=jax.ShapeDtypeStruct((B, ATTN_WIDTH, S), BF16),
        scratch_shapes=[pltpu.VMEM((TK, G * TQ), F32),
                        pltpu.VMEM((VX_ROWS, S), BF16)],
        compiler_params=pltpu.CompilerParams(
            dimension_semantics=("parallel", "parallel"), vmem_limit_bytes=VMEM_LIMIT),
        name="attn",
    )(bound.astype(F32), q, k, vt)


def _log_sigmoid(x):
    return jnp.minimum(x, 0.0) - jnp.log(1.0 + jnp.exp(-jnp.abs(x)))


def _mlstm_kernel(qs_ref, kt_ref, v_ref, o_ref, grow_ref, ng_ref, y_ref,
                  rows_ref, col_ref, hf_ref, hb_ref, ct_ref):
    S = v_ref.shape[1]
    L = CHUNK
    assert L == LANES, "chunk tiles are square (L, 128) tiles"
    nc = S // L
    Dh = MLSTM_HEAD_DIM
    RB = 512

    ri = lax.broadcasted_iota(jnp.int32, (L, L), 0)
    ci = lax.broadcasted_iota(jnp.int32, (L, L), 1)
    lower = ci <= ri
    upper = ri <= ci
    tril_b = lower.astype(BF16)
    triu_b = upper.astype(BF16)

    def split3(x):
        t1 = x.astype(BF16).astype(F32)
        e1 = x - t1
        t2 = e1.astype(BF16).astype(F32)
        t3 = (e1 - t2).astype(BF16).astype(F32)
        return t1, t2, t3

    NR = GATE_SLOTS * nc
    heads = qs_ref.shape[2] // Dh
    tri2 = jnp.concatenate([triu_b, tril_b], axis=1)
    rj = lax.broadcasted_iota(jnp.int32, (NR, L), 0) & (GATE_SLOTS - 1)
    lane = lax.broadcasted_iota(jnp.int32, (NR, L), 1)
    is_fw = rj == 0
    is_r = (rj & 5) == 0
    src_rows = (1, 0, 4, 3, 2, 6)
    kk = lax.broadcasted_iota(jnp.int32, (4 * GATE_SLOTS, LANES), 0)
    blk = lax.broadcasted_iota(jnp.int32, (4 * GATE_SLOTS, LANES), 1)
    want = jnp.full((4 * GATE_SLOTS, LANES), -1, jnp.int32)
    for q, j in enumerate(src_rows):
        want = jnp.where(blk == q, j, want)
    sel = ((kk & (GATE_SLOTS - 1)) == jnp.where(kk < 3 * GATE_SLOTS, want, -1)).astype(BF16)
    zeros8 = jnp.zeros((GATE_SLOTS, L), F32)
    for hh in range(heads):
        graw = jnp.concatenate([grow_ref[0, hh, :, c * L:(c + 1) * L] for c in range(nc)], axis=0)
        ps = sum(jnp.dot(t.astype(BF16), tri2, preferred_element_type=F32)
                 for t in split3(_log_sigmoid(graw) * LOG2E))
        pre, suf = ps[:, :L], ps[:, L:]
        rr = graw * LOG2E - jnp.where(is_fw, pltpu.roll(pre, NR - 1, 0), pltpu.roll(suf, NR - 1, 0))
        x = jnp.where(is_r, rr, -jnp.inf)
        sh = 1
        while sh < L:
            fwd = jnp.where(lane >= sh, pltpu.roll(x, sh, 1), -jnp.inf)
            bwd = jnp.where(lane < L - sh, pltpu.roll(x, L - sh, 1), -jnp.inf)
            x = jnp.maximum(x, jnp.where(is_fw, fwd, bwd))
            sh *= 2
        quant = jnp.where(is_r, rr, jnp.where(rj == 1, pre, jnp.where(rj == 3, suf, jnp.where(
            (rj & 5) == 4, pltpu.roll(x, 4, 0), 0.0))))
        terms = split3(quant)
        stacked = []
        for c in range(nc):
            rows = slice(c * GATE_SLOTS, (c + 1) * GATE_SLOTS)
            rows_ref[hh, :, c * L:(c + 1) * L] = quant[rows]
            stacked.append(jnp.concatenate([t[rows] for t in terms] + [zeros8], axis=0))
        col_ref[hh] = _tn_dot(jnp.concatenate(stacked, axis=1).astype(BF16), sel)

    ones_blk = jnp.ones((L, LANES), BF16)
    ct_ref[...] = jnp.zeros_like(ct_ref)

    def direction(c, m, hh, d_idx, mask, g_row, out_ref):
        c0 = pl.multiple_of(c * L, L)
        hcols = slice(hh * Dh, (hh + 1) * Dh)
        qc = qs_ref[0, pl.ds(c0, L), hcols]
        ktc = kt_ref[0, hcols, pl.ds(c0, L)]
        vc = v_ref[0, pl.ds(c0, L), hcols]
        r_row = rows_ref[hh, 2 * d_idx:2 * d_idx + 1, pl.ds(c0, L)]
        colblk = col_ref[hh, pl.ds(c0, L), :]
        b = jnp.broadcast_to(colblk[:, 3 * d_idx:3 * d_idx + 1], (L, LANES))
        r = jnp.broadcast_to(colblk[:, 3 * d_idx + 1:3 * d_idx + 2], (L, LANES))
        cmax = jnp.broadcast_to(colblk[:, 3 * d_idx + 2:3 * d_idx + 3], (L, LANES))
        rmax = cmax[g_row:g_row + 1]
        p = jnp.exp2(jnp.where(mask, r_row - cmax, -jnp.inf))
        s = (jnp.dot(qc, ktc, preferred_element_type=F32) * p).astype(BF16)
        intra = jnp.dot(s, jnp.concatenate([vc, ones_blk], axis=1), preferred_element_type=F32)
        w = jnp.exp2(r - rmax)
        wv = jnp.concatenate([(w * vc.astype(F32)).astype(BF16), w.astype(BF16)], axis=1)
        upd = jnp.dot(ktc, wv, preferred_element_type=F32)
        mc = jnp.maximum(m, cmax)
        w_inter = jnp.exp2(m - mc)
        w_intra = jnp.exp2(cmax - mc)
        ct = ct_ref[2 * hh + d_idx]
        inter = jnp.dot(qc, ct.astype(BF16), preferred_element_type=F32)
        num = w_inter * inter[:, :Dh] + w_intra * intra[:, :Dh]
        den = w_inter * inter[:, Dh:] + w_intra * intra[:, Dh:]
        out_ref[hh, pl.ds(c0, L), :] = num / jnp.maximum(jnp.abs(den), jnp.exp2(-(b + mc)))
        g = b[g_row:g_row + 1]
        mu = jnp.maximum(m, rmax)
        decay = jnp.exp2(m - mu)
        gain = jnp.exp2(rmax - mu)
        ct_ref[2 * hh + d_idx] = jnp.concatenate([decay, decay], axis=1) * ct \
            + jnp.concatenate([gain, gain], axis=1) * upd
        return g + mu

    def step(j, carry):
        out = []
        for hh in range(heads):
            out.append(direction(j, carry[2 * hh], hh, 0, lower, L - 1, hf_ref))
            out.append(direction(nc - 1 - j, carry[2 * hh + 1], hh, 1, upper, 0, hb_ref))
        return tuple(out)

    zero_row = jnp.zeros((1, LANES), F32)
    lax.fori_loop(0, nc, step, (zero_row,) * (2 * heads), unroll=MLSTM_LOOP_UNROLL)

    for hh in range(heads):
        hcols = slice(hh * Dh, (hh + 1) * Dh)
        ng = ng_ref[:, hcols]
        for r in range(S // RB):
            sl = slice(r * RB, (r + 1) * RB)
            hsum = hf_ref[hh, sl] + hb_ref[hh, sl]
            hn = hsum * lax.rsqrt(jnp.mean(hsum * hsum, axis=-1, keepdims=True) + EPS)
            hn = hn * ng
            y_ref[0, sl, hcols] = (hn * jax.nn.sigmoid(o_ref[0, sl, hcols])).astype(BF16)


def _mlstm(qm, ktm, vm, om, grow, norm_g):
    B, S, _ = vm.shape
    H = MLSTM_HEADS
    Dh = MLSTM_HEAD_DIM
    hps = MLSTM_HEADS_PER_STEP
    W = hps * Dh
    seq = pl.BlockSpec((1, S, W), lambda b, h: (b, 0, h))
    return pl.pallas_call(
        _mlstm_kernel,
        grid=(B, H // hps),
        in_specs=[
            seq,
            pl.BlockSpec((1, W, S), lambda b, h: (b, h, 0)),
            seq, seq,
            pl.BlockSpec((1, hps, GATE_SLOTS, S), lambda b, h: (b, h, 0, 0)),
            pl.BlockSpec((1, W), lambda b, h: (0, h)),
        ],
        out_specs=pl.BlockSpec((1, S, W), lambda b, h: (b, 0, h)),
        out_shape=jax.ShapeDtypeStruct((B, S, MLSTM_WIDTH), BF16),
        scratch_shapes=[
            pltpu.VMEM((hps, GATE_SLOTS, S), F32),
            pltpu.VMEM((hps, S, LANES), F32),
            pltpu.VMEM((hps, S, Dh), F32),
            pltpu.VMEM((hps, S, Dh), F32),
            pltpu.VMEM((2 * hps, Dh, 2 * Dh), F32),
        ],
        compiler_params=pltpu.CompilerParams(
            dimension_semantics=("parallel", "parallel"), vmem_limit_bytes=VMEM_LIMIT),
        name="mlstm",
    )(qm, ktm, vm, om, grow, norm_g.reshape(1, -1))


def _outffn_kernel(x_ref, mod_ref, yat_ref, ym_ref, wo_ref, gf_ref, wg_ref, wu_ref, wd_ref, fg_ref, o_ref):
    D = D_MODEL
    x = x_ref[0]
    mod = mod_ref[0]
    g1 = mod[:, 2 * D:3 * D]
    sh2 = mod[:, 3 * D:4 * D]
    sc2 = mod[:, 4 * D:5 * D]
    g2 = mod[:, 5 * D:6 * D]
    tm = x.shape[0]
    halves = [slice(0, tm // 2), slice(tm // 2, tm)]
    x1s = []
    for rows in halves:
        mix = _tn_dot(yat_ref[0, :, rows], wo_ref[0:ATTN_WIDTH, :]) \
            + jnp.dot(ym_ref[0, rows, :], wo_ref[ATTN_WIDTH:MIX_WIDTH, :], preferred_element_type=F32)
        x1s.append(x[rows] + g1 * mix)
    for rows, x1 in zip(halves, x1s):
        ms = jnp.mean(x1 * x1, axis=-1, keepdims=True)
        h = (x1 * lax.rsqrt(ms + EPS)) * gf_ref[...]
        hb = (h * (1.0 + sc2) + sh2).astype(BF16)
        gate = jnp.dot(hb, wg_ref[...], preferred_element_type=F32)
        up = jnp.dot(hb, wu_ref[...], preferred_element_type=F32)
        act = (gate * jax.nn.sigmoid(gate) * up).astype(BF16)
        ffn = jnp.dot(act, wd_ref[...], preferred_element_type=F32)
        x2 = x1 + g2 * ffn
        ms2 = jnp.mean(x2 * x2, axis=-1, keepdims=True)
        o_ref[0, rows, :] = (x2 * lax.rsqrt(ms2 + EPS)) * fg_ref[...]


def _outffn(x, mod3, yat, ym, w_out, norm_ffn_g, w_gate, w_up, w_down, final_g):
    B, S, D = x.shape
    tm = TM_FFN
    F = w_gate.shape[1]
    const = lambda shape: pl.BlockSpec(shape, lambda b, i: (0,) * len(shape), pipeline_mode=pl.Buffered(1))
    return pl.pallas_call(
        _outffn_kernel,
        grid=(B, S // tm),
        in_specs=[
            pl.BlockSpec((1, tm, D), lambda b, i: (b, i, 0)),
            pl.BlockSpec((1, 1, 6 * D), lambda b, i: (b, 0, 0)),
            pl.BlockSpec((1, ATTN_WIDTH, tm), lambda b, i: (b, 0, i)),
            pl.BlockSpec((1, tm, MLSTM_WIDTH), lambda b, i: (b, i, 0)),
            const((MIX_WIDTH, D)),
            const((1, D)),
            const((D, F)),
            const((D, F)),
            const((F, D)),
            const((1, D)),
        ],
        out_specs=pl.BlockSpec((1, tm, D), lambda b, i: (b, i, 0)),
        out_shape=jax.ShapeDtypeStruct((B, S, D), F32),
        compiler_params=pltpu.CompilerParams(
            dimension_semantics=("parallel", "parallel"), vmem_limit_bytes=VMEM_LIMIT),
        name="outffn",
    )(x, mod3, yat, ym, w_out.astype(BF16), norm_ffn_g.reshape(1, D), w_gate.astype(BF16),
      w_up.astype(BF16), w_down.astype(BF16), final_g.reshape(1, D))


def kernel(x, c, w_ada, b_ada, norm_mix_g, w_in, q_norm_g, k_norm_g, conv_w, conv_b, gate_b,
           mlstm_norm_g, w_out, norm_ffn_g, w_gate, w_up, w_down, final_norm_g):
    assert w_ada.shape[0] == 1, "the final RMSNorm is fused into the single layer's last kernel"
    B = x.shape[0]
    l = 0
    mod3 = _ada(c, w_ada[l], b_ada[l]).reshape(B, 1, -1)
    q, k, vt, qm, ktm, vm, om, grow = _inproj(x, mod3, norm_mix_g[l], w_in[l], gate_b[l],
                                              q_norm_g[l], k_norm_g[l], conv_w[l], conv_b[l])
    yat = _attention(q, k, vt, q_norm_g[l], k_norm_g[l])
    ym = _mlstm(qm, ktm, vm, om, grow, mlstm_norm_g[l])
    return _outffn(x, mod3, yat, ym, w_out[l], norm_ffn_g[l], w_gate[l], w_up[l], w_down[l], final_norm_g)
```

```python
import jax
import jax.numpy as jnp
import numpy as np
from jax import lax
from jax.experimental import pallas as pl
from jax.experimental.pallas import tpu as pltpu

F32 = jnp.float32
BF16 = jnp.bfloat16

D_MODEL = 1024
GRID_W = 64
EPS = 1e-6
ATTN_HEADS = 8
ATTN_KV_HEADS = 2
ATTN_GROUP = ATTN_HEADS // ATTN_KV_HEADS
ATTN_HEAD_DIM = 64
ATTN_WIDTH = ATTN_HEADS * ATTN_HEAD_DIM
KV_WIDTH = ATTN_KV_HEADS * ATTN_HEAD_DIM
ROPE_THETA = 10000.0
MLSTM_HEADS = 4
MLSTM_HEAD_DIM = 128
MLSTM_WIDTH = MLSTM_HEADS * MLSTM_HEAD_DIM
CONV_WIDTH = 5
N_GATES = 4 * MLSTM_HEADS
MIX_WIDTH = ATTN_WIDTH + MLSTM_WIDTH

LANES = 128
GATE_SLOTS = 8

TM_IN = 1024
TM_FFN = 512
TQ = 128
TK = 256
F32_SUBLANES = 8
BF16_SUBLANES = 16
LOG2E = float(np.log2(np.e))
VX_ROWS = ATTN_HEAD_DIM + BF16_SUBLANES
CHUNK = 128
MLSTM_HEADS_PER_STEP = 1
MLSTM_LOOP_UNROLL = 8
MLSTM_STAGES = 3
Q_SCALE = ATTN_HEAD_DIM ** -0.5 * float(np.log2(np.e))
VMEM_LIMIT = 56 * 1024 * 1024
SAFE_SCORE_BOUND = 50.0
TK_BOUNDED = 256
SCORE_TILES_AHEAD = 2
QBLOCKS_PER_BODY = 8
SCORE_BOUND_MARGIN = 1.02


def _tn_dot(a, b):
    return lax.dot_general(a, b, (((0,), (0,)), ((), ())), preferred_element_type=F32)


def _ada_kernel(c_ref, w_ref, b_ref, o_ref):
    c = c_ref[...]
    cond = c * jax.nn.sigmoid(c)
    o_ref[...] = jnp.dot(cond, w_ref[...], preferred_element_type=F32,
                         precision=lax.Precision.HIGHEST) + b_ref[...]


def _ada(c, w_ada, b_ada):
    B, D = c.shape
    n = w_ada.shape[1] // D
    return pl.pallas_call(
        _ada_kernel,
        grid=(n,),
        in_specs=[pl.BlockSpec((B, D), lambda j: (0, 0)),
                  pl.BlockSpec((D, D), lambda j: (0, j)),
                  pl.BlockSpec((1, D), lambda j: (0, j))],
        out_specs=pl.BlockSpec((B, D), lambda j: (0, j)),
        out_shape=jax.ShapeDtypeStruct((B, n * D), F32),
        name="ada",
    )(c, w_ada, b_ada.reshape(1, -1))


_C_Q = 0
_C_K = _C_Q + ATTN_WIDTH
_C_V = _C_K + KV_WIDTH
_C_QKM = _C_V + KV_WIDTH
_C_VM = _C_QKM + 2 * MLSTM_WIDTH
_C_OM = _C_VM + MLSTM_WIDTH
_C_G = _C_OM + MLSTM_WIDTH
_C_END = _C_G + LANES


def _inproj_kernel(x_ref, xp_ref, xn_ref, mod_ref, g_ref, w_ref, gb_ref, qg_ref, kg_ref, cos_ref, sa_ref, sb_ref,
                   bd_ref, cw_ref, cb_ref,
                   q_ref, k_ref, vt_ref, qm_ref, ktm_ref, vm_ref, om_ref, grow_ref, cpad_ref):
    D = D_MODEL
    mod = mod_ref[0]
    sh1 = mod[:, 0:D]
    sc1 = mod[:, D:2 * D]

    def normed(x):
        ms = jnp.mean(x * x, axis=-1, keepdims=True)
        h = (x * lax.rsqrt(ms + EPS)) * g_ref[...]
        return (h * (1.0 + sc1) + sh1).astype(BF16)

    hb = normed(x_ref[0])
    hb_halo = normed(jnp.concatenate([xp_ref[0], xn_ref[0]], axis=0))

    cosf = cos_ref[...]
    sa = sa_ref[...]
    sb = sb_ref[...]
    bd = bd_ref[...]

    def norm_rope(xs, gain, scale):
        ss = jnp.dot((xs * xs).astype(BF16), bd, preferred_element_type=F32)
        xn = (xs * lax.rsqrt(ss + EPS)) * gain
        out = xn * cosf + pltpu.roll(xn, LANES - 16, 1) * sa + pltpu.roll(xn, 16, 1) * sb
        return out * scale

    hd = ATTN_HEAD_DIM
    qkv = jnp.dot(hb, w_ref[:, _C_Q:_C_QKM], preferred_element_type=F32)
    for j in range(ATTN_WIDTH // LANES):
        qt = norm_rope(qkv[:, j * LANES:(j + 1) * LANES], qg_ref[...], Q_SCALE).T.astype(BF16)
        q_ref[0, 2 * j] = qt[:hd]
        q_ref[0, 2 * j + 1] = qt[hd:]
    ks = norm_rope(qkv[:, _C_K:_C_K + LANES], kg_ref[...], 1.0).astype(BF16)
    k_ref[0, 0] = ks[:, :hd]
    k_ref[0, 1] = ks[:, hd:]
    vt = qkv[:, _C_V:_C_V + LANES].T.astype(BF16)
    vt_ref[0, 0] = vt[:hd]
    vt_ref[0, 1] = vt[hd:]
    tm = hb.shape[0]
    halo = F32_SUBLANES
    qk_pre = jnp.dot(jnp.concatenate([hb, hb_halo], axis=0), w_ref[:, _C_QKM:_C_VM],
                     preferred_element_type=F32)
    i = pl.program_id(1)
    before = jnp.where(i > 0, qk_pre[tm:tm + halo], 0.0)
    after = jnp.where(i < pl.num_programs(1) - 1, qk_pre[tm + halo:], 0.0)
    rb = LANES
    for j0 in range(0, 2 * MLSTM_WIDTH, LANES):
        cols = slice(j0, j0 + LANES)
        slab = j0 // LANES
        cpad_ref[slab, 0:halo] = before[:, cols]
        cpad_ref[slab, halo:halo + tm] = qk_pre[0:tm, cols]
        cpad_ref[slab, halo + tm:2 * halo + tm] = after[:, cols]
        for r0 in range(0, tm, rb):
            acc = cb_ref[:, cols]
            for t in range(CONV_WIDTH):
                off = halo + r0 + t - CONV_WIDTH // 2
                acc = acc + cpad_ref[slab, off:off + rb, :] * cw_ref[t:t + 1, cols]
            y = acc * jax.nn.sigmoid(acc)
            if j0 < MLSTM_WIDTH:
                qm_ref[0, r0:r0 + rb, cols] = y.astype(BF16)
            else:
                ktm_ref[0, j0 - MLSTM_WIDTH:j0 - MLSTM_WIDTH + LANES, r0:r0 + rb] = \
                    (y * MLSTM_HEAD_DIM ** -0.5).T.astype(BF16)
    vm_ref[0] = jnp.dot(hb, w_ref[:, _C_VM:_C_OM], preferred_element_type=F32).astype(BF16)
    om_ref[0] = jnp.dot(hb, w_ref[:, _C_OM:_C_G], preferred_element_type=F32)
    gates = jnp.dot(hb, w_ref[:, _C_G:_C_END], preferred_element_type=F32) + gb_ref[...]
    gt = gates.T
    H = MLSTM_HEADS
    unused = jnp.zeros((GATE_SLOTS - 4, tm), F32)
    for h in range(H):
        grow_ref[0, h] = jnp.concatenate([gt[t * H + h:t * H + h + 1] for t in range(4)] + [unused], axis=0)


def _rope_tables(S):
    rows = S // GRID_W
    row = np.repeat(np.arange(rows, dtype=np.float64), GRID_W)
    col = np.tile(np.arange(GRID_W, dtype=np.float64), rows)
    axis_dim = ATTN_HEAD_DIM // 2
    inv_freq = (ROPE_THETA ** (-np.arange(0, axis_dim, 2, dtype=np.float32) / axis_dim)).astype(np.float32)
    ar = (row[:, None].astype(np.float32) * inv_freq[None, :]).astype(np.float64)
    ac = (col[:, None].astype(np.float32) * inv_freq[None, :]).astype(np.float64)
    cr, sr, cc, sc = np.cos(ar), np.sin(ar), np.cos(ac), np.sin(ac)
    z = np.zeros_like(sr)
    cos_h = np.concatenate([cr, cr, cc, cc], axis=-1)
    sa_h = np.concatenate([-sr, z, -sc, z], axis=-1)
    sb_h = np.concatenate([z, sr, z, sc], axis=-1)
    two = lambda a: jnp.asarray(np.concatenate([a, a], axis=-1), F32)
    return two(cos_h), two(sa_h), two(sb_h)


def _inproj(x, mod3, norm_g, w_in, gate_b, q_g, k_g, conv_w, conv_b):
    B, S, D = x.shape
    tm = TM_IN
    H = MLSTM_HEADS
    w = jnp.pad(w_in.astype(BF16), ((0, 0), (0, _C_END - w_in.shape[1])))
    gb = jnp.pad(gate_b.reshape(1, -1), ((0, 0), (0, LANES - N_GATES)))
    cosf, sa, sb = _rope_tables(S)
    qg2 = jnp.concatenate([q_g, q_g]).reshape(1, LANES)
    kg2 = jnp.concatenate([k_g, k_g]).reshape(1, LANES)
    blk = np.kron(np.eye(LANES // ATTN_HEAD_DIM), np.ones((ATTN_HEAD_DIM, ATTN_HEAD_DIM))) / ATTN_HEAD_DIM
    bd = jnp.asarray(blk, BF16)

    const = lambda shape: pl.BlockSpec(shape, lambda b, i: (0,) * len(shape))
    hd = ATTN_HEAD_DIM
    out_shape = (
        jax.ShapeDtypeStruct((B, ATTN_HEADS, hd, S), BF16),
        jax.ShapeDtypeStruct((B, ATTN_KV_HEADS, S, hd), BF16),
        jax.ShapeDtypeStruct((B, ATTN_KV_HEADS, hd, S), BF16),
        jax.ShapeDtypeStruct((B, S, MLSTM_WIDTH), BF16),
        jax.ShapeDtypeStruct((B, MLSTM_WIDTH, S), BF16),
        jax.ShapeDtypeStruct((B, S, MLSTM_WIDTH), BF16),
        jax.ShapeDtypeStruct((B, S, MLSTM_WIDTH), F32),
        jax.ShapeDtypeStruct((B, H, GATE_SLOTS, S), F32),
    )
    out_specs = (
        pl.BlockSpec((1, ATTN_HEADS, hd, tm), lambda b, i: (b, 0, 0, i)),
        pl.BlockSpec((1, ATTN_KV_HEADS, tm, hd), lambda b, i: (b, 0, i, 0)),
        pl.BlockSpec((1, ATTN_KV_HEADS, hd, tm), lambda b, i: (b, 0, 0, i)),
        pl.BlockSpec((1, tm, MLSTM_WIDTH), lambda b, i: (b, i, 0)),
        pl.BlockSpec((1, MLSTM_WIDTH, tm), lambda b, i: (b, 0, i)),
        pl.BlockSpec((1, tm, MLSTM_WIDTH), lambda b, i: (b, i, 0)),
        pl.BlockSpec((1, tm, MLSTM_WIDTH), lambda b, i: (b, i, 0)),
        pl.BlockSpec((1, H, GATE_SLOTS, tm), lambda b, i: (b, 0, 0, i)),
    )
    halo = F32_SUBLANES
    tiles = tm // halo
    in_specs = [
        pl.BlockSpec((1, tm, D), lambda b, i: (b, i, 0)),
        pl.BlockSpec((1, halo, D), lambda b, i: (b, jnp.maximum(i * tiles - 1, 0), 0)),
        pl.BlockSpec((1, halo, D), lambda b, i: (b, jnp.minimum((i + 1) * tiles, S // halo - 1), 0)),
        pl.BlockSpec((1, 1, 6 * D), lambda b, i: (b, 0, 0)),
        const((1, D)),
        const((D, _C_END)),
        const((1, LANES)),
        const((1, LANES)),
        const((1, LANES)),
        pl.BlockSpec((tm, LANES), lambda b, i: (i, 0)),
        pl.BlockSpec((tm, LANES), lambda b, i: (i, 0)),
        pl.BlockSpec((tm, LANES), lambda b, i: (i, 0)),
        const((LANES, LANES)),
        const((CONV_WIDTH, 2 * MLSTM_WIDTH)),
        const((1, 2 * MLSTM_WIDTH)),
    ]
    return pl.pallas_call(
        _inproj_kernel,
        grid=(B, S // tm),
        in_specs=in_specs,
        out_specs=out_specs,
        out_shape=out_shape,
        scratch_shapes=[
            pltpu.VMEM((2 * MLSTM_WIDTH // LANES, tm + 2 * halo, LANES), F32)],
        compiler_params=pltpu.CompilerParams(
            dimension_semantics=("parallel", "parallel"), vmem_limit_bytes=VMEM_LIMIT),
        name="inproj",
    )(x, x, x, mod3, norm_g.reshape(1, D), w, gb, qg2, kg2, cosf, sa, sb, bd, conv_w, conv_b.reshape(1, -1))


def _attn_kernel(bound_ref, q_ref, k_ref, vt_ref, o_ref, s0_ref, vx_ref):
    G = ATTN_GROUP
    hd = ATTN_HEAD_DIM
    S = k_ref.shape[2]
    R = G * TQ
    nk = S // TK
    nq = S // TQ

    def load_q(qi):
        r0 = pl.multiple_of(qi * TQ, TQ)
        return jnp.concatenate([q_ref[0, g, :, pl.ds(r0, TQ)] for g in range(G)], axis=1)

    def scores(qb, kj, tk=TK):
        kb = k_ref[0, 0, kj * tk:(kj + 1) * tk, :]
        return jnp.dot(kb, qb, preferred_element_type=F32)

    def update(kj, s, m, acc):
        mn = jnp.maximum(m, jnp.max(s, axis=0, keepdims=True))
        alpha = jnp.exp2(m - mn)
        p = jnp.exp2(s - mn).astype(BF16)
        vb = vx_ref[:, kj * TK:(kj + 1) * TK]
        acc = alpha * acc + jnp.dot(vb, p, preferred_element_type=F32)
        return mn, acc

    def finish(qi, num, den):
        o = (num / den).astype(BF16)
        r0 = pl.multiple_of(qi * TQ, TQ)
        for g in range(G):
            o_ref[0, g * hd:(g + 1) * hd, pl.ds(r0, TQ)] = o[:, g * TQ:(g + 1) * TQ]

    bound = bound_ref[0]

    @pl.when(bound <= SAFE_SCORE_BOUND)
    def _():
        tk = TK_BOUNDED
        group = QBLOCKS_PER_BODY
        assert nq % group == 0 and S % tk == 0

        def qgroup(gi, carry):
            qbs = [load_q(gi * group + b) for b in range(group)]
            stream = [(b, kj) for b in range(group) for kj in range(S // tk)]
            tiles = {i: scores(qbs[b], kj, tk) for i, (b, kj) in enumerate(stream[:SCORE_TILES_AHEAD])}
            acc = l = None
            for i, (b, kj) in enumerate(stream):
                nxt = i + SCORE_TILES_AHEAD
                if nxt < len(stream):
                    tiles[nxt] = scores(qbs[stream[nxt][0]], stream[nxt][1], tk)
                if kj == 0:
                    acc = jnp.zeros((hd, R), F32)
                    l = jnp.zeros((F32_SUBLANES, R), F32)
                p = jnp.exp2(tiles.pop(i) - bound)
                l = l + jnp.sum(p.reshape(tk // F32_SUBLANES, F32_SUBLANES, R), axis=0)
                acc = acc + jnp.dot(vt_ref[0, 0, :, kj * tk:(kj + 1) * tk], p.astype(BF16),
                                    preferred_element_type=F32)
                if kj == S // tk - 1:
                    finish(gi * group + b, acc, jnp.sum(l, axis=0, keepdims=True))
            return carry

        lax.fori_loop(0, nq // group, qgroup, 0)

    @pl.when(bound > SAFE_SCORE_BOUND)
    def _():
        vx_ref[0:hd] = vt_ref[0, 0]
        extra = lax.broadcasted_iota(jnp.int32, (VX_ROWS - hd, S), 0)
        vx_ref[hd:VX_ROWS] = (extra == 0).astype(BF16)
        s0_ref[...] = scores(load_q(0), 0)

        def qblock(qi, carry):
            qb = load_q(qi)
            ss = [scores(qb, kj) for kj in range(1, nk)]
            s_next = scores(load_q(jnp.minimum(qi + 1, nq - 1)), 0)
            m = jnp.full((1, R), -jnp.inf, F32)
            acc = jnp.zeros((VX_ROWS, R), F32)
            m, acc = update(0, s0_ref[...], m, acc)
            for kj in range(1, nk):
                m, acc = update(kj, ss[kj - 1], m, acc)
            s0_ref[...] = s_next
            finish(qi, acc[0:hd], acc[hd:hd + 1])
            return carry

        lax.fori_loop(0, nq, qblock, 0)


def _attention(q, k, vt, q_g, k_g):
    B, _, hd, S = q.shape
    G = ATTN_GROUP
    bound = (SCORE_BOUND_MARGIN * hd * Q_SCALE * jnp.max(jnp.abs(q_g)) * jnp.max(jnp.abs(k_g))).reshape(1)
    return pl.pallas_call(
        _attn_kernel,
        grid=(B, ATTN_KV_HEADS),
        in_specs=[
            pl.BlockSpec(memory_space=pltpu.SMEM),
            pl.BlockSpec((1, G, hd, S), lambda b, kv: (b, kv, 0, 0)),
            pl.BlockSpec((1, 1, S, hd), lambda b, kv: (b, kv, 0, 0)),
            pl.BlockSpec((1, 1, hd, S), lambda b, kv: (b, kv, 0, 0)),
        ],
        out_specs=pl.BlockSpec((1, G * hd, S), lambda b, kv: (b, kv, 0)),
        out_shape=jax.ShapeDtypeStruct((B, ATTN_WIDTH, S), BF16),
        scratch_shapes=[pltpu.VMEM((TK, G * TQ), F32),
                        pltpu.VMEM((VX_ROWS, S), BF16)],
        compiler_params=pltpu.CompilerParams(
            dimension_semantics=("parallel", "parallel"), vmem_limit_bytes=VMEM_LIMIT),
        name="attn",
    )(bound.astype(F32), q, k, vt)


def _log_sigmoid(x):
    return jnp.minimum(x, 0.0) - jnp.log(1.0 + jnp.exp(-jnp.abs(x)))


def _mlstm_kernel(qs_ref, kt_ref, v_ref, o_ref, grow_ref, ng_ref, y_ref,
                  rows_ref, col_ref, hf_ref, hb_ref, ct_ref):
    S = v_ref.shape[1]
    L = CHUNK
    assert L == LANES, "chunk tiles are square (L, 128) tiles"
    nc = S // L
    Dh = MLSTM_HEAD_DIM
    RB = 512

    ri = lax.broadcasted_iota(jnp.int32, (L, L), 0)
    ci = lax.broadcasted_iota(jnp.int32, (L, L), 1)
    lower = ci <= ri
    upper = ri <= ci
    tril_b = lower.astype(BF16)
    triu_b = upper.astype(BF16)

    def split3(x):
        t1 = x.astype(BF16).astype(F32)
        e1 = x - t1
        t2 = e1.astype(BF16).astype(F32)
        t3 = (e1 - t2).astype(BF16).astype(F32)
        return t1, t2, t3

    NR = GATE_SLOTS * nc
    heads = qs_ref.shape[2] // Dh
    tri2 = jnp.concatenate([triu_b, tril_b], axis=1)
    rj = lax.broadcasted_iota(jnp.int32, (NR, L), 0) & (GATE_SLOTS - 1)
    lane = lax.broadcasted_iota(jnp.int32, (NR, L), 1)
    is_fw = rj == 0
    is_r = (rj & 5) == 0
    src_rows = (1, 0, 4, 3, 2, 6)
    kk = lax.broadcasted_iota(jnp.int32, (4 * GATE_SLOTS, LANES), 0)
    blk = lax.broadcasted_iota(jnp.int32, (4 * GATE_SLOTS, LANES), 1)
    want = jnp.full((4 * GATE_SLOTS, LANES), -1, jnp.int32)
    for q, j in enumerate(src_rows):
        want = jnp.where(blk == q, j, want)
    sel = ((kk & (GATE_SLOTS - 1)) == jnp.where(kk < 3 * GATE_SLOTS, want, -1)).astype(BF16)
    zeros8 = jnp.zeros((GATE_SLOTS, L), F32)
    for hh in range(heads):
        graw = jnp.concatenate([grow_ref[0, hh, :, c * L:(c + 1) * L] for c in range(nc)], axis=0)
        ps = sum(jnp.dot(t.astype(BF16), tri2, preferred_element_type=F32)
                 for t in split3(_log_sigmoid(graw) * LOG2E))
        pre, suf = ps[:, :L], ps[:, L:]
        rr = graw * LOG2E - jnp.where(is_fw, pltpu.roll(pre, NR - 1, 0), pltpu.roll(suf, NR - 1, 0))
        x = jnp.where(is_r, rr, -jnp.inf)
        sh = 1
        while sh < L:
            fwd = jnp.where(lane >= sh, pltpu.roll(x, sh, 1), -jnp.inf)
            bwd = jnp.where(lane < L - sh, pltpu.roll(x, L - sh, 1), -jnp.inf)
            x = jnp.maximum(x, jnp.where(is_fw, fwd, bwd))
            sh *= 2
        quant = jnp.where(is_r, rr, jnp.where(rj == 1, pre, jnp.where(rj == 3, suf, jnp.where(
            (rj & 5) == 4, pltpu.roll(x, 4, 0), 0.0))))
        terms = split3(quant)
        stacked = []
        for c in range(nc):
            rows = slice(c * GATE_SLOTS, (c + 1) * GATE_SLOTS)
            rows_ref[hh, :, c * L:(c + 1) * L] = quant[rows]
            stacked.append(jnp.concatenate([t[rows] for t in terms] + [zeros8], axis=0))
        col_ref[hh] = _tn_dot(jnp.concatenate(stacked, axis=1).astype(BF16), sel)

    ones_blk = jnp.ones((L, LANES), BF16)
    ct_ref[...] = jnp.zeros_like(ct_ref)

    def direction(c, m, hh, d_idx, mask, g_row, out_ref):
        c0 = pl.multiple_of(c * L, L)
        hcols = slice(hh * Dh, (hh + 1) * Dh)
        qc = qs_ref[0, pl.ds(c0, L), hcols]
        ktc = kt_ref[0, hcols, pl.ds(c0, L)]
        vc = v_ref[0, pl.ds(c0, L), hcols]
        r_row = rows_ref[hh, 2 * d_idx:2 * d_idx + 1, pl.ds(c0, L)]
        colblk = col_ref[hh, pl.ds(c0, L), :]
        b = jnp.broadcast_to(colblk[:, 3 * d_idx:3 * d_idx + 1], (L, LANES))
        r = jnp.broadcast_to(colblk[:, 3 * d_idx + 1:3 * d_idx + 2], (L, LANES))
        cmax = jnp.broadcast_to(colblk[:, 3 * d_idx + 2:3 * d_idx + 3], (L, LANES))
        qk = jnp.dot(qc, ktc, preferred_element_type=F32)
        ct = ct_ref[2 * hh + d_idx]
        inter = jnp.dot(qc, ct.astype(BF16), preferred_element_type=F32)
        yield None
        rmax = cmax[g_row:g_row + 1]
        w = jnp.exp2(r - rmax)
        wv = jnp.concatenate([(w * vc.astype(F32)).astype(BF16), w.astype(BF16)], axis=1)
        upd = jnp.dot(ktc, wv, preferred_element_type=F32)
        p = jnp.exp2(jnp.where(mask, r_row - cmax, -jnp.inf))
        s = (qk * p).astype(BF16)
        intra = jnp.dot(s, jnp.concatenate([vc, ones_blk], axis=1), preferred_element_type=F32)
        yield None
        mc = jnp.maximum(m, cmax)
        w_inter = jnp.exp2(m - mc)
        w_intra = jnp.exp2(cmax - mc)
        num = w_inter * inter[:, :Dh] + w_intra * intra[:, :Dh]
        den = w_inter * inter[:, Dh:] + w_intra * intra[:, Dh:]
        out_ref[hh, pl.ds(c0, L), :] = num / jnp.maximum(jnp.abs(den), jnp.exp2(-(b + mc)))
        g = b[g_row:g_row + 1]
        mu = jnp.maximum(m, rmax)
        decay = jnp.exp2(m - mu)
        gain = jnp.exp2(rmax - mu)
        ct_ref[2 * hh + d_idx] = jnp.concatenate([decay, decay], axis=1) * ct \
            + jnp.concatenate([gain, gain], axis=1) * upd
        yield g + mu

    def step(j, carry):
        gens = []
        for hh in range(heads):
            gens.append(direction(j, carry[2 * hh], hh, 0, lower, L - 1, hf_ref))
            gens.append(direction(nc - 1 - j, carry[2 * hh + 1], hh, 1, upper, 0, hb_ref))
        for _ in range(MLSTM_STAGES - 1):
            for gen in gens:
                next(gen)
        return tuple(next(gen) for gen in gens)

    zero_row = jnp.zeros((1, LANES), F32)
    lax.fori_loop(0, nc, step, (zero_row,) * (2 * heads), unroll=MLSTM_LOOP_UNROLL)

    for hh in range(heads):
        hcols = slice(hh * Dh, (hh + 1) * Dh)
        ng = ng_ref[:, hcols]
        for r in range(S // RB):
            sl = slice(r * RB, (r + 1) * RB)
            hsum = hf_ref[hh, sl] + hb_ref[hh, sl]
            hn = hsum * lax.rsqrt(jnp.mean(hsum * hsum, axis=-1, keepdims=True) + EPS)
            hn = hn * ng
            y_ref[0, sl, hcols] = (hn * jax.nn.sigmoid(o_ref[0, sl, hcols])).astype(BF16)


def _mlstm(qm, ktm, vm, om, grow, norm_g):
    B, S, _ = vm.shape
    H = MLSTM_HEADS
    Dh = MLSTM_HEAD_DIM
    hps = MLSTM_HEADS_PER_STEP
    W = hps * Dh
    seq = pl.BlockSpec((1, S, W), lambda b, h: (b, 0, h))
    return pl.pallas_call(
        _mlstm_kernel,
        grid=(B, H // hps),
        in_specs=[
            seq,
            pl.BlockSpec((1, W, S), lambda b, h: (b, h, 0)),
            seq, seq,
            pl.BlockSpec((1, hps, GATE_SLOTS, S), lambda b, h: (b, h, 0, 0)),
            pl.BlockSpec((1, W), lambda b, h: (0, h)),
        ],
        out_specs=pl.BlockSpec((1, S, W), lambda b, h: (b, 0, h)),
        out_shape=jax.ShapeDtypeStruct((B, S, MLSTM_WIDTH), BF16),
        scratch_shapes=[
            pltpu.VMEM((hps, GATE_SLOTS, S), F32),
            pltpu.VMEM((hps, S, LANES), F32),
            pltpu.VMEM((hps, S, Dh), F32),
            pltpu.VMEM((hps, S, Dh), F32),
            pltpu.VMEM((2 * hps, Dh, 2 * Dh), F32),
        ],
        compiler_params=pltpu.CompilerParams(
            dimension_semantics=("parallel", "parallel"), vmem_limit_bytes=VMEM_LIMIT),
        name="mlstm",
    )(qm, ktm, vm, om, grow, norm_g.reshape(1, -1))


def _outffn_kernel(x_ref, mod_ref, yat_ref, ym_ref, wo_ref, gf_ref, wg_ref, wu_ref, wd_ref, fg_ref, o_ref):
    D = D_MODEL
    x = x_ref[0]
    mod = mod_ref[0]
    g1 = mod[:, 2 * D:3 * D]
    sh2 = mod[:, 3 * D:4 * D]
    sc2 = mod[:, 4 * D:5 * D]
    g2 = mod[:, 5 * D:6 * D]
    tm = x.shape[0]
    halves = [slice(0, tm // 2), slice(tm // 2, tm)]
    x1s = []
    for rows in halves:
        mix = _tn_dot(yat_ref[0, :, rows], wo_ref[0:ATTN_WIDTH, :]) \
            + jnp.dot(ym_ref[0, rows, :], wo_ref[ATTN_WIDTH:MIX_WIDTH, :], preferred_element_type=F32)
        x1s.append(x[rows] + g1 * mix)
    for rows, x1 in zip(halves, x1s):
        ms = jnp.mean(x1 * x1, axis=-1, keepdims=True)
        h = (x1 * lax.rsqrt(ms + EPS)) * gf_ref[...]
        hb = (h * (1.0 + sc2) + sh2).astype(BF16)
        gate = jnp.dot(hb, wg_ref[...], preferred_element_type=F32)
        up = jnp.dot(hb, wu_ref[...], preferred_element_type=F32)
        act = (gate * jax.nn.sigmoid(gate) * up).astype(BF16)
        ffn = jnp.dot(act, wd_ref[...], preferred_element_type=F32)
        x2 = x1 + g2 * ffn
        ms2 = jnp.mean(x2 * x2, axis=-1, keepdims=True)
        o_ref[0, rows, :] = (x2 * lax.rsqrt(ms2 + EPS)) * fg_ref[...]


def _outffn(x, mod3, yat, ym, w_out, norm_ffn_g, w_gate, w_up, w_down, final_g):
    B, S, D = x.shape
    tm = TM_FFN
    F = w_gate.shape[1]
    const = lambda shape: pl.BlockSpec(shape, lambda b, i: (0,) * len(shape), pipeline_mode=pl.Buffered(1))
    return pl.pallas_call(
        _outffn_kernel,
        grid=(B, S // tm),
        in_specs=[
            pl.BlockSpec((1, tm, D), lambda b, i: (b, i, 0)),
            pl.BlockSpec((1, 1, 6 * D), lambda b, i: (b, 0, 0)),
            pl.BlockSpec((1, ATTN_WIDTH, tm), lambda b, i: (b, 0, i)),
            pl.BlockSpec((1, tm, MLSTM_WIDTH), lambda b, i: (b, i, 0)),
            const((MIX_WIDTH, D)),
            const((1, D)),
            const((D, F)),
            const((D, F)),
            const((F, D)),
            const((1, D)),
        ],
        out_specs=pl.BlockSpec((1, tm, D), lambda b, i: (b, i, 0)),
        out_shape=jax.ShapeDtypeStruct((B, S, D), F32),
        compiler_params=pltpu.CompilerParams(
            dimension_semantics=("parallel", "parallel"), vmem_limit_bytes=VMEM_LIMIT),
        name="outffn",
    )(x, mod3, yat, ym, w_out.astype(BF16), norm_ffn_g.reshape(1, D), w_gate.astype(BF16),
      w_up.astype(BF16), w_down.astype(BF16), final_g.reshape(1, D))


def kernel(x, c, w_ada, b_ada, norm_mix_g, w_in, q_norm_g, k_norm_g, conv_w, conv_b, gate_b,
           mlstm_norm_g, w_out, norm_ffn_g, w_gate, w_up, w_down, final_norm_g):
    assert w_ada.shape[0] == 1, "the final RMSNorm is fused into the single layer's last kernel"
    B = x.shape[0]
    l = 0
    mod3 = _ada(c, w_ada[l], b_ada[l]).reshape(B, 1, -1)
    q, k, vt, qm, ktm, vm, om, grow = _inproj(x, mod3, norm_mix_g[l], w_in[l], gate_b[l],
                                              q_norm_g[l], k_norm_g[l], conv_w[l], conv_b[l])
    yat = _attention(q, k, vt, q_norm_g[l], k_norm_g[l])
    ym = _mlstm(qm, ktm, vm, om, grow, mlstm_norm_g[l])
    return _outffn(x, mod3, yat, ym, w_out[l], norm_ffn_g[l], w_gate[l], w_up[l], w_down[l], final_norm_g)
```

```python
import jax
import jax.numpy as jnp
import numpy as np
from jax import lax
from jax.experimental import pallas as pl
from jax.experimental.pallas import tpu as pltpu

F32 = jnp.float32
BF16 = jnp.bfloat16

D_MODEL = 1024
GRID_W = 64
EPS = 1e-6
ATTN_HEADS = 8
ATTN_KV_HEADS = 2
ATTN_GROUP = ATTN_HEADS // ATTN_KV_HEADS
ATTN_HEAD_DIM = 64
ATTN_WIDTH = ATTN_HEADS * ATTN_HEAD_DIM
KV_WIDTH = ATTN_KV_HEADS * ATTN_HEAD_DIM
ROPE_THETA = 10000.0
MLSTM_HEADS = 4
MLSTM_HEAD_DIM = 128
MLSTM_WIDTH = MLSTM_HEADS * MLSTM_HEAD_DIM
CONV_WIDTH = 5
N_GATES = 4 * MLSTM_HEADS
MIX_WIDTH = ATTN_WIDTH + MLSTM_WIDTH

LANES = 128
GATE_SLOTS = 8

TM_IN = 1024
TM_FFN = 512
TQ = 128
TK = 256
F32_SUBLANES = 8
BF16_SUBLANES = 16
LOG2E = float(np.log2(np.e))
VX_ROWS = ATTN_HEAD_DIM + BF16_SUBLANES
CHUNK = 128
MLSTM_HEADS_PER_STEP = 1
MLSTM_LOOP_UNROLL = 8
MLSTM_STAGES = 3
Q_SCALE = ATTN_HEAD_DIM ** -0.5 * float(np.log2(np.e))
VMEM_LIMIT = 56 * 1024 * 1024
SAFE_SCORE_BOUND = 50.0
TK_BOUNDED = 256
SCORE_TILES_AHEAD = 2
QBLOCKS_PER_BODY = 8
SCORE_BOUND_MARGIN = 1.02


def _tn_dot(a, b):
    return lax.dot_general(a, b, (((0,), (0,)), ((), ())), preferred_element_type=F32)


def _ada_kernel(c_ref, w_ref, b_ref, o_ref):
    c = c_ref[...]
    cond = c * jax.nn.sigmoid(c)
    o_ref[...] = jnp.dot(cond, w_ref[...], preferred_element_type=F32,
                         precision=lax.Precision.HIGHEST) + b_ref[...]


def _ada(c, w_ada, b_ada):
    B, D = c.shape
    n = w_ada.shape[1] // D
    return pl.pallas_call(
        _ada_kernel,
        grid=(n,),
        in_specs=[pl.BlockSpec((B, D), lambda j: (0, 0)),
                  pl.BlockSpec((D, D), lambda j: (0, j)),
                  pl.BlockSpec((1, D), lambda j: (0, j))],
        out_specs=pl.BlockSpec((B, D), lambda j: (0, j)),
        out_shape=jax.ShapeDtypeStruct((B, n * D), F32),
        name="ada",
    )(c, w_ada, b_ada.reshape(1, -1))


_C_Q = 0
_C_K = _C_Q + ATTN_WIDTH
_C_V = _C_K + KV_WIDTH
_C_QKM = _C_V + KV_WIDTH
_C_VM = _C_QKM + 2 * MLSTM_WIDTH
_C_OM = _C_VM + MLSTM_WIDTH
_C_G = _C_OM + MLSTM_WIDTH
_C_END = _C_G + LANES


def _inproj_kernel(x_ref, xp_ref, xn_ref, mod_ref, g_ref, w_ref, gb_ref, qg_ref, kg_ref, cos_ref, sa_ref, sb_ref,
                   bd_ref, cw_ref, cb_ref,
                   q_ref, k_ref, vt_ref, qm_ref, ktm_ref, vm_ref, om_ref, grow_ref, cpad_ref):
    D = D_MODEL
    mod = mod_ref[0]
    sh1 = mod[:, 0:D]
    sc1 = mod[:, D:2 * D]

    def normed(x):
        ms = jnp.mean(x * x, axis=-1, keepdims=True)
        h = (x * lax.rsqrt(ms + EPS)) * g_ref[...]
        return (h * (1.0 + sc1) + sh1).astype(BF16)

    hb = normed(x_ref[0])
    hb_halo = normed(jnp.concatenate([xp_ref[0], xn_ref[0]], axis=0))

    cosf = cos_ref[...]
    sa = sa_ref[...]
    sb = sb_ref[...]
    bd = bd_ref[...]

    def norm_rope(xs, gain, scale):
        ss = jnp.dot((xs * xs).astype(BF16), bd, preferred_element_type=F32)
        xn = (xs * lax.rsqrt(ss + EPS)) * gain
        out = xn * cosf + pltpu.roll(xn, LANES - 16, 1) * sa + pltpu.roll(xn, 16, 1) * sb
        return out * scale

    hd = ATTN_HEAD_DIM
    qkv = jnp.dot(hb, w_ref[:, _C_Q:_C_QKM], preferred_element_type=F32)
    for j in range(ATTN_WIDTH // LANES):
        qt = norm_rope(qkv[:, j * LANES:(j + 1) * LANES], qg_ref[...], Q_SCALE).T.astype(BF16)
        q_ref[0, 2 * j] = qt[:hd]
        q_ref[0, 2 * j + 1] = qt[hd:]
    ks = norm_rope(qkv[:, _C_K:_C_K + LANES], kg_ref[...], 1.0).astype(BF16)
    k_ref[0, 0] = ks[:, :hd]
    k_ref[0, 1] = ks[:, hd:]
    vt = qkv[:, _C_V:_C_V + LANES].T.astype(BF16)
    vt_ref[0, 0] = vt[:hd]
    vt_ref[0, 1] = vt[hd:]
    tm = hb.shape[0]
    halo = F32_SUBLANES
    qk_pre = jnp.dot(jnp.concatenate([hb, hb_halo], axis=0), w_ref[:, _C_QKM:_C_VM],
                     preferred_element_type=F32)
    i = pl.program_id(1)
    before = jnp.where(i > 0, qk_pre[tm:tm + halo], 0.0)
    after = jnp.where(i < pl.num_programs(1) - 1, qk_pre[tm + halo:], 0.0)
    rb = LANES
    for j0 in range(0, 2 * MLSTM_WIDTH, LANES):
        cols = slice(j0, j0 + LANES)
        slab = j0 // LANES
        cpad_ref[slab, 0:halo] = before[:, cols]
        cpad_ref[slab, halo:halo + tm] = qk_pre[0:tm, cols]
        cpad_ref[slab, halo + tm:2 * halo + tm] = after[:, cols]
        for r0 in range(0, tm, rb):
            acc = cb_ref[:, cols]
            for t in range(CONV_WIDTH):
                off = halo + r0 + t - CONV_WIDTH // 2
                acc = acc + cpad_ref[slab, off:off + rb, :] * cw_ref[t:t + 1, cols]
            y = acc * jax.nn.sigmoid(acc)
            if j0 < MLSTM_WIDTH:
                qm_ref[0, r0:r0 + rb, cols] = y.astype(BF16)
            else:
                ktm_ref[0, j0 - MLSTM_WIDTH:j0 - MLSTM_WIDTH + LANES, r0:r0 + rb] = \
                    (y * MLSTM_HEAD_DIM ** -0.5).T.astype(BF16)
    vm_ref[0] = jnp.dot(hb, w_ref[:, _C_VM:_C_OM], preferred_element_type=F32).astype(BF16)
    om_ref[0] = jnp.dot(hb, w_ref[:, _C_OM:_C_G], preferred_element_type=F32)
    gates = jnp.dot(hb, w_ref[:, _C_G:_C_END], preferred_element_type=F32) + gb_ref[...]
    gt = gates.T
    H = MLSTM_HEADS
    unused = jnp.zeros((GATE_SLOTS - 4, tm), F32)
    for h in range(H):
        grow_ref[0, h] = jnp.concatenate([gt[t * H + h:t * H + h + 1] for t in range(4)] + [unused], axis=0)


def _rope_tables(S):
    rows = S // GRID_W
    row = np.repeat(np.arange(rows, dtype=np.float64), GRID_W)
    col = np.tile(np.arange(GRID_W, dtype=np.float64), rows)
    axis_dim = ATTN_HEAD_DIM // 2
    inv_freq = (ROPE_THETA ** (-np.arange(0, axis_dim, 2, dtype=np.float32) / axis_dim)).astype(np.float32)
    ar = (row[:, None].astype(np.float32) * inv_freq[None, :]).astype(np.float64)
    ac = (col[:, None].astype(np.float32) * inv_freq[None, :]).astype(np.float64)
    cr, sr, cc, sc = np.cos(ar), np.sin(ar), np.cos(ac), np.sin(ac)
    z = np.zeros_like(sr)
    cos_h = np.concatenate([cr, cr, cc, cc], axis=-1)
    sa_h = np.concatenate([-sr, z, -sc, z], axis=-1)
    sb_h = np.concatenate([z, sr, z, sc], axis=-1)
    two = lambda a: jnp.asarray(np.concatenate([a, a], axis=-1), F32)
    return two(cos_h), two(sa_h), two(sb_h)


def _inproj(x, mod3, norm_g, w_in, gate_b, q_g, k_g, conv_w, conv_b):
    B, S, D = x.shape
    tm = TM_IN
    H = MLSTM_HEADS
    w = jnp.pad(w_in.astype(BF16), ((0, 0), (0, _C_END - w_in.shape[1])))
    gb = jnp.pad(gate_b.reshape(1, -1), ((0, 0), (0, LANES - N_GATES)))
    cosf, sa, sb = _rope_tables(S)
    qg2 = jnp.concatenate([q_g, q_g]).reshape(1, LANES)
    kg2 = jnp.concatenate([k_g, k_g]).reshape(1, LANES)
    blk = np.kron(np.eye(LANES // ATTN_HEAD_DIM), np.ones((ATTN_HEAD_DIM, ATTN_HEAD_DIM))) / ATTN_HEAD_DIM
    bd = jnp.asarray(blk, BF16)

    const = lambda shape: pl.BlockSpec(shape, lambda b, i: (0,) * len(shape))
    hd = ATTN_HEAD_DIM
    out_shape = (
        jax.ShapeDtypeStruct((B, ATTN_HEADS, hd, S), BF16),
        jax.ShapeDtypeStruct((B, ATTN_KV_HEADS, S, hd), BF16),
        jax.ShapeDtypeStruct((B, ATTN_KV_HEADS, hd, S), BF16),
        jax.ShapeDtypeStruct((B, S, MLSTM_WIDTH), BF16),
        jax.ShapeDtypeStruct((B, MLSTM_WIDTH, S), BF16),
        jax.ShapeDtypeStruct((B, S, MLSTM_WIDTH), BF16),
        jax.ShapeDtypeStruct((B, S, MLSTM_WIDTH), F32),
        jax.ShapeDtypeStruct((B, H, GATE_SLOTS, S), F32),
    )
    out_specs = (
        pl.BlockSpec((1, ATTN_HEADS, hd, tm), lambda b, i: (b, 0, 0, i)),
        pl.BlockSpec((1, ATTN_KV_HEADS, tm, hd), lambda b, i: (b, 0, i, 0)),
        pl.BlockSpec((1, ATTN_KV_HEADS, hd, tm), lambda b, i: (b, 0, 0, i)),
        pl.BlockSpec((1, tm, MLSTM_WIDTH), lambda b, i: (b, i, 0)),
        pl.BlockSpec((1, MLSTM_WIDTH, tm), lambda b, i: (b, 0, i)),
        pl.BlockSpec((1, tm, MLSTM_WIDTH), lambda b, i: (b, i, 0)),
        pl.BlockSpec((1, tm, MLSTM_WIDTH), lambda b, i: (b, i, 0)),
        pl.BlockSpec((1, H, GATE_SLOTS, tm), lambda b, i: (b, 0, 0, i)),
    )
    halo = F32_SUBLANES
    tiles = tm // halo
    in_specs = [
        pl.BlockSpec((1, tm, D), lambda b, i: (b, i, 0)),
        pl.BlockSpec((1, halo, D), lambda b, i: (b, jnp.maximum(i * tiles - 1, 0), 0)),
        pl.BlockSpec((1, halo, D), lambda b, i: (b, jnp.minimum((i + 1) * tiles, S // halo - 1), 0)),
        pl.BlockSpec((1, 1, 6 * D), lambda b, i: (b, 0, 0)),
        const((1, D)),
        const((D, _C_END)),
        const((1, LANES)),
        const((1, LANES)),
        const((1, LANES)),
        pl.BlockSpec((tm, LANES), lambda b, i: (i, 0)),
        pl.BlockSpec((tm, LANES), lambda b, i: (i, 0)),
        pl.BlockSpec((tm, LANES), lambda b, i: (i, 0)),
        const((LANES, LANES)),
        const((CONV_WIDTH, 2 * MLSTM_WIDTH)),
        const((1, 2 * MLSTM_WIDTH)),
    ]
    return pl.pallas_call(
        _inproj_kernel,
        grid=(B, S // tm),
        in_specs=in_specs,
        out_specs=out_specs,
        out_shape=out_shape,
        scratch_shapes=[
            pltpu.VMEM((2 * MLSTM_WIDTH // LANES, tm + 2 * halo, LANES), F32)],
        compiler_params=pltpu.CompilerParams(
            dimension_semantics=("parallel", "parallel"), vmem_limit_bytes=VMEM_LIMIT),
        name="inproj",
    )(x, x, x, mod3, norm_g.reshape(1, D), w, gb, qg2, kg2, cosf, sa, sb, bd, conv_w, conv_b.reshape(1, -1))


def _attn_kernel(bound_ref, q_ref, k_ref, vt_ref, o_ref, s0_ref, vx_ref):
    G = ATTN_GROUP
    hd = ATTN_HEAD_DIM
    S = k_ref.shape[2]
    R = G * TQ
    nk = S // TK
    nq = S // TQ

    def load_q(qi):
        r0 = pl.multiple_of(qi * TQ, TQ)
        return jnp.concatenate([q_ref[0, g, :, pl.ds(r0, TQ)] for g in range(G)], axis=1)

    def scores(qb, kj, tk=TK):
        kb = k_ref[0, 0, kj * tk:(kj + 1) * tk, :]
        return jnp.dot(kb, qb, preferred_element_type=F32)

    def update(kj, s, m, acc):
        mn = jnp.maximum(m, jnp.max(s, axis=0, keepdims=True))
        alpha = jnp.exp2(m - mn)
        p = jnp.exp2(s - mn).astype(BF16)
        vb = vx_ref[:, kj * TK:(kj + 1) * TK]
        acc = alpha * acc + jnp.dot(vb, p, preferred_element_type=F32)
        return mn, acc

    def finish(qi, num, den):
        o = (num / den).astype(BF16)
        r0 = pl.multiple_of(qi * TQ, TQ)
        for g in range(G):
            o_ref[0, g * hd:(g + 1) * hd, pl.ds(r0, TQ)] = o[:, g * TQ:(g + 1) * TQ]

    bound = bound_ref[0]

    @pl.when(bound <= SAFE_SCORE_BOUND)
    def _():
        tk = TK_BOUNDED
        group = QBLOCKS_PER_BODY
        assert nq % group == 0 and S % tk == 0

        def qgroup(gi, carry):
            qbs = [load_q(gi * group + b) for b in range(group)]
            stream = [(b, kj) for b in range(group) for kj in range(S // tk)]
            tiles = {i: scores(qbs[b], kj, tk) for i, (b, kj) in enumerate(stream[:SCORE_TILES_AHEAD])}
            acc = l = None
            for i, (b, kj) in enumerate(stream):
                nxt = i + SCORE_TILES_AHEAD
                if nxt < len(stream):
                    tiles[nxt] = scores(qbs[stream[nxt][0]], stream[nxt][1], tk)
                if kj == 0:
                    acc = jnp.zeros((hd, R), F32)
                    l = jnp.zeros((F32_SUBLANES, R), F32)
                p = jnp.exp2(tiles.pop(i) - bound)
                l = l + jnp.sum(p.reshape(tk // F32_SUBLANES, F32_SUBLANES, R), axis=0)
                acc = acc + jnp.dot(vt_ref[0, 0, :, kj * tk:(kj + 1) * tk], p.astype(BF16),
                                    preferred_element_type=F32)
                if kj == S // tk - 1:
                    finish(gi * group + b, acc, jnp.sum(l, axis=0, keepdims=True))
            return carry

        lax.fori_loop(0, nq // group, qgroup, 0)

    @pl.when(bound > SAFE_SCORE_BOUND)
    def _():
        vx_ref[0:hd] = vt_ref[0, 0]
        extra = lax.broadcasted_iota(jnp.int32, (VX_ROWS - hd, S), 0)
        vx_ref[hd:VX_ROWS] = (extra == 0).astype(BF16)
        s0_ref[...] = scores(load_q(0), 0)

        def qblock(qi, carry):
            qb = load_q(qi)
            ss = [scores(qb, kj) for kj in range(1, nk)]
            s_next = scores(load_q(jnp.minimum(qi + 1, nq - 1)), 0)
            m = jnp.full((1, R), -jnp.inf, F32)
            acc = jnp.zeros((VX_ROWS, R), F32)
            m, acc = update(0, s0_ref[...], m, acc)
            for kj in range(1, nk):
                m, acc = update(kj, ss[kj - 1], m, acc)
            s0_ref[...] = s_next
            finish(qi, acc[0:hd], acc[hd:hd + 1])
            return carry

        lax.fori_loop(0, nq, qblock, 0)


def _attention(q, k, vt, q_g, k_g):
    B, _, hd, S = q.shape
    G = ATTN_GROUP
    bound = (SCORE_BOUND_MARGIN * hd * Q_SCALE * jnp.max(jnp.abs(q_g)) * jnp.max(jnp.abs(k_g))).reshape(1)
    return pl.pallas_call(
        _attn_kernel,
        grid=(B, ATTN_KV_HEADS),
        in_specs=[
            pl.BlockSpec(memory_space=pltpu.SMEM),
            pl.BlockSpec((1, G, hd, S), lambda b, kv: (b, kv, 0, 0)),
            pl.BlockSpec((1, 1, S, hd), lambda b, kv: (b, kv, 0, 0)),
            pl.BlockSpec((1, 1, hd, S), lambda b, kv: (b, kv, 0, 0)),
        ],
        out_specs=pl.BlockSpec((1, G * hd, S), lambda b, kv: (b, kv, 0)),
        out_shape=jax.ShapeDtypeStruct((B, ATTN_WIDTH, S), BF16),
        scratch_shapes=[pltpu.VMEM((TK, G * TQ), F32),
                        pltpu.VMEM((VX_ROWS, S), BF16)],
        compiler_params=pltpu.CompilerParams(
            dimension_semantics=("parallel", "parallel"), vmem_limit_bytes=VMEM_LIMIT),
        name="attn",
    )(bound.astype(F32), q, k, vt)


def _log_sigmoid(x):
    return jnp.minimum(x, 0.0) - jnp.log(1.0 + jnp.exp(-jnp.abs(x)))


def _mlstm_kernel(qs_ref, kt_ref, v_ref, o_ref, grow_ref, grow_next_ref, ng_ref, y_ref,
                  rows_ref, col_ref, hf_ref, hb_ref, ct_ref):
    S = v_ref.shape[1]
    L = CHUNK
    assert L == LANES, "chunk tiles are square (L, 128) tiles"
    nc = S // L
    Dh = MLSTM_HEAD_DIM
    RB = 512

    ri = lax.broadcasted_iota(jnp.int32, (L, L), 0)
    ci = lax.broadcasted_iota(jnp.int32, (L, L), 1)
    lower = ci <= ri
    upper = ri <= ci
    tril_b = lower.astype(BF16)
    triu_b = upper.astype(BF16)

    def split3(x):
        t1 = x.astype(BF16).astype(F32)
        e1 = x - t1
        t2 = e1.astype(BF16).astype(F32)
        t3 = (e1 - t2).astype(BF16).astype(F32)
        return t1, t2, t3

    NR = GATE_SLOTS * nc
    heads = qs_ref.shape[2] // Dh
    tri2 = jnp.concatenate([triu_b, tril_b], axis=1)
    rj = lax.broadcasted_iota(jnp.int32, (NR, L), 0) & (GATE_SLOTS - 1)
    lane = lax.broadcasted_iota(jnp.int32, (NR, L), 1)
    is_fw = rj == 0
    is_r = (rj & 5) == 0
    src_rows = (1, 0, 4, 3, 2, 6)
    kk = lax.broadcasted_iota(jnp.int32, (4 * GATE_SLOTS, LANES), 0)
    blk = lax.broadcasted_iota(jnp.int32, (4 * GATE_SLOTS, LANES), 1)
    want = jnp.full((4 * GATE_SLOTS, LANES), -1, jnp.int32)
    for q, j in enumerate(src_rows):
        want = jnp.where(blk == q, j, want)
    sel = ((kk & (GATE_SLOTS - 1)) == jnp.where(kk < 3 * GATE_SLOTS, want, -1)).astype(BF16)
    zeros8 = jnp.zeros((GATE_SLOTS, L), F32)

    def prepare_head(src_ref, slot, hh):
        graw = jnp.concatenate([src_ref[0, hh, :, c * L:(c + 1) * L] for c in range(nc)], axis=0)
        ps = sum(jnp.dot(t.astype(BF16), tri2, preferred_element_type=F32)
                 for t in split3(_log_sigmoid(graw) * LOG2E))
        pre, suf = ps[:, :L], ps[:, L:]
        rr = graw * LOG2E - jnp.where(is_fw, pltpu.roll(pre, NR - 1, 0), pltpu.roll(suf, NR - 1, 0))
        x = jnp.where(is_r, rr, -jnp.inf)
        sh = 1
        while sh < L:
            fwd = jnp.where(lane >= sh, pltpu.roll(x, sh, 1), -jnp.inf)
            bwd = jnp.where(lane < L - sh, pltpu.roll(x, L - sh, 1), -jnp.inf)
            x = jnp.maximum(x, jnp.where(is_fw, fwd, bwd))
            sh *= 2
        quant = jnp.where(is_r, rr, jnp.where(rj == 1, pre, jnp.where(rj == 3, suf, jnp.where(
            (rj & 5) == 4, pltpu.roll(x, 4, 0), 0.0))))
        terms = split3(quant)
        stacked = []
        for c in range(nc):
            rows = slice(c * GATE_SLOTS, (c + 1) * GATE_SLOTS)
            rows_ref[slot, hh, :, c * L:(c + 1) * L] = quant[rows]
            stacked.append(jnp.concatenate([t[rows] for t in terms] + [zeros8], axis=0))
        col_ref[slot, hh] = _tn_dot(jnp.concatenate(stacked, axis=1).astype(BF16), sel)

    step_id = pl.program_id(0) * pl.num_programs(1) + pl.program_id(1)
    slot = lax.rem(step_id, 2)

    @pl.when(step_id == 0)
    def _():
        for hh in range(heads):
            prepare_head(grow_ref, 0, hh)

    ones_blk = jnp.ones((L, LANES), BF16)
    ct_ref[...] = jnp.zeros_like(ct_ref)

    def direction(c, m, hh, d_idx, mask, g_row, out_ref):
        c0 = pl.multiple_of(c * L, L)
        hcols = slice(hh * Dh, (hh + 1) * Dh)
        qc = qs_ref[0, pl.ds(c0, L), hcols]
        ktc = kt_ref[0, hcols, pl.ds(c0, L)]
        vc = v_ref[0, pl.ds(c0, L), hcols]
        r_row = rows_ref[slot, hh, 2 * d_idx:2 * d_idx + 1, pl.ds(c0, L)]
        colblk = col_ref[slot, hh, pl.ds(c0, L), :]
        b = jnp.broadcast_to(colblk[:, 3 * d_idx:3 * d_idx + 1], (L, LANES))
        r = jnp.broadcast_to(colblk[:, 3 * d_idx + 1:3 * d_idx + 2], (L, LANES))
        cmax = jnp.broadcast_to(colblk[:, 3 * d_idx + 2:3 * d_idx + 3], (L, LANES))
        qk = jnp.dot(qc, ktc, preferred_element_type=F32)
        ct = ct_ref[2 * hh + d_idx]
        inter = jnp.dot(qc, ct.astype(BF16), preferred_element_type=F32)
        yield None
        rmax = cmax[g_row:g_row + 1]
        w = jnp.exp2(r - rmax)
        wv = jnp.concatenate([(w * vc.astype(F32)).astype(BF16), w.astype(BF16)], axis=1)
        upd = jnp.dot(ktc, wv, preferred_element_type=F32)
        p = jnp.exp2(jnp.where(mask, r_row - cmax, -jnp.inf))
        s = (qk * p).astype(BF16)
        intra = jnp.dot(s, jnp.concatenate([vc, ones_blk], axis=1), preferred_element_type=F32)
        yield None
        mc = jnp.maximum(m, cmax)
        w_inter = jnp.exp2(m - mc)
        w_intra = jnp.exp2(cmax - mc)
        num = w_inter * inter[:, :Dh] + w_intra * intra[:, :Dh]
        den = w_inter * inter[:, Dh:] + w_intra * intra[:, Dh:]
        out_ref[hh, pl.ds(c0, L), :] = num / jnp.maximum(jnp.abs(den), jnp.exp2(-(b + mc)))
        g = b[g_row:g_row + 1]
        mu = jnp.maximum(m, rmax)
        decay = jnp.exp2(m - mu)
        gain = jnp.exp2(rmax - mu)
        ct_ref[2 * hh + d_idx] = jnp.concatenate([decay, decay], axis=1) * ct \
            + jnp.concatenate([gain, gain], axis=1) * upd
        yield g + mu

    def step(j, carry):
        gens = []
        for hh in range(heads):
            gens.append(direction(j, carry[2 * hh], hh, 0, lower, L - 1, hf_ref))
            gens.append(direction(nc - 1 - j, carry[2 * hh + 1], hh, 1, upper, 0, hb_ref))
        for _ in range(MLSTM_STAGES - 1):
            for gen in gens:
                next(gen)
        return tuple(next(gen) for gen in gens)

    zero_row = jnp.zeros((1, LANES), F32)
    lax.fori_loop(0, nc, step, (zero_row,) * (2 * heads), unroll=MLSTM_LOOP_UNROLL)

    for hh in range(heads):
        prepare_head(grow_next_ref, 1 - slot, hh)

    for hh in range(heads):
        hcols = slice(hh * Dh, (hh + 1) * Dh)
        ng = ng_ref[:, hcols]
        for r in range(S // RB):
            sl = slice(r * RB, (r + 1) * RB)
            hsum = hf_ref[hh, sl] + hb_ref[hh, sl]
            hn = hsum * lax.rsqrt(jnp.mean(hsum * hsum, axis=-1, keepdims=True) + EPS)
            hn = hn * ng
            y_ref[0, sl, hcols] = (hn * jax.nn.sigmoid(o_ref[0, sl, hcols])).astype(BF16)


def _mlstm(qm, ktm, vm, om, grow, norm_g):
    B, S, _ = vm.shape
    H = MLSTM_HEADS
    Dh = MLSTM_HEAD_DIM
    hps = MLSTM_HEADS_PER_STEP
    W = hps * Dh
    seq = pl.BlockSpec((1, S, W), lambda b, h: (b, 0, h))
    steps = H // hps

    def next_gates(b, h):
        nxt = jnp.minimum(b * steps + h + 1, B * steps - 1)
        return (nxt // steps, nxt % steps, 0, 0)

    return pl.pallas_call(
        _mlstm_kernel,
        grid=(B, steps),
        in_specs=[
            seq,
            pl.BlockSpec((1, W, S), lambda b, h: (b, h, 0)),
            seq, seq,
            pl.BlockSpec((1, hps, GATE_SLOTS, S), lambda b, h: (b, h, 0, 0)),
            pl.BlockSpec((1, hps, GATE_SLOTS, S), next_gates),
            pl.BlockSpec((1, W), lambda b, h: (0, h)),
        ],
        out_specs=pl.BlockSpec((1, S, W), lambda b, h: (b, 0, h)),
        out_shape=jax.ShapeDtypeStruct((B, S, MLSTM_WIDTH), BF16),
        scratch_shapes=[
            pltpu.VMEM((2, hps, GATE_SLOTS, S), F32),
            pltpu.VMEM((2, hps, S, LANES), F32),
            pltpu.VMEM((hps, S, Dh), F32),
            pltpu.VMEM((hps, S, Dh), F32),
            pltpu.VMEM((2 * hps, Dh, 2 * Dh), F32),
        ],
        compiler_params=pltpu.CompilerParams(
            dimension_semantics=("arbitrary", "arbitrary"), vmem_limit_bytes=VMEM_LIMIT),
        name="mlstm",
    )(qm, ktm, vm, om, grow, grow, norm_g.reshape(1, -1))


def _outffn_kernel(x_ref, mod_ref, yat_ref, ym_ref, wo_ref, gf_ref, wg_ref, wu_ref, wd_ref, fg_ref, o_ref):
    D = D_MODEL
    x = x_ref[0]
    mod = mod_ref[0]
    g1 = mod[:, 2 * D:3 * D]
    sh2 = mod[:, 3 * D:4 * D]
    sc2 = mod[:, 4 * D:5 * D]
    g2 = mod[:, 5 * D:6 * D]
    tm = x.shape[0]
    halves = [slice(0, tm // 2), slice(tm // 2, tm)]
    x1s = []
    for rows in halves:
        mix = _tn_dot(yat_ref[0, :, rows], wo_ref[0:ATTN_WIDTH, :]) \
            + jnp.dot(ym_ref[0, rows, :], wo_ref[ATTN_WIDTH:MIX_WIDTH, :], preferred_element_type=F32)
        x1s.append(x[rows] + g1 * mix)
    for rows, x1 in zip(halves, x1s):
        ms = jnp.mean(x1 * x1, axis=-1, keepdims=True)
        h = (x1 * lax.rsqrt(ms + EPS)) * gf_ref[...]
        hb = (h * (1.0 + sc2) + sh2).astype(BF16)
        gate = jnp.dot(hb, wg_ref[...], preferred_element_type=F32)
        up = jnp.dot(hb, wu_ref[...], preferred_element_type=F32)
        act = (gate * jax.nn.sigmoid(gate) * up).astype(BF16)
        ffn = jnp.dot(act, wd_ref[...], preferred_element_type=F32)
        x2 = x1 + g2 * ffn
        ms2 = jnp.mean(x2 * x2, axis=-1, keepdims=True)
        o_ref[0, rows, :] = (x2 * lax.rsqrt(ms2 + EPS)) * fg_ref[...]


def _outffn(x, mod3, yat, ym, w_out, norm_ffn_g, w_gate, w_up, w_down, final_g):
    B, S, D = x.shape
    tm = TM_FFN
    F = w_gate.shape[1]
    const = lambda shape: pl.BlockSpec(shape, lambda b, i: (0,) * len(shape), pipeline_mode=pl.Buffered(1))
    return pl.pallas_call(
        _outffn_kernel,
        grid=(B, S // tm),
        in_specs=[
            pl.BlockSpec((1, tm, D), lambda b, i: (b, i, 0)),
            pl.BlockSpec((1, 1, 6 * D), lambda b, i: (b, 0, 0)),
            pl.BlockSpec((1, ATTN_WIDTH, tm), lambda b, i: (b, 0, i)),
            pl.BlockSpec((1, tm, MLSTM_WIDTH), lambda b, i: (b, i, 0)),
            const((MIX_WIDTH, D)),
            const((1, D)),
            const((D, F)),
            const((D, F)),
            const((F, D)),
            const((1, D)),
        ],
        out_specs=pl.BlockSpec((1, tm, D), lambda b, i: (b, i, 0)),
        out_shape=jax.ShapeDtypeStruct((B, S, D), F32),
        compiler_params=pltpu.CompilerParams(
            dimension_semantics=("parallel", "parallel"), vmem_limit_bytes=VMEM_LIMIT),
        name="outffn",
    )(x, mod3, yat, ym, w_out.astype(BF16), norm_ffn_g.reshape(1, D), w_gate.astype(BF16),
      w_up.astype(BF16), w_down.astype(BF16), final_g.reshape(1, D))


def kernel(x, c, w_ada, b_ada, norm_mix_g, w_in, q_norm_g, k_norm_g, conv_w, conv_b, gate_b,
           mlstm_norm_g, w_out, norm_ffn_g, w_gate, w_up, w_down, final_norm_g):
    assert w_ada.shape[0] == 1, "the final RMSNorm is fused into the single layer's last kernel"
    B = x.shape[0]
    l = 0
    mod3 = _ada(c, w_ada[l], b_ada[l]).reshape(B, 1, -1)
    q, k, vt, qm, ktm, vm, om, grow = _inproj(x, mod3, norm_mix_g[l], w_in[l], gate_b[l],
                                              q_norm_g[l], k_norm_g[l], conv_w[l], conv_b[l])
    yat = _attention(q, k, vt, q_norm_g[l], k_norm_g[l])
    ym = _mlstm(qm, ktm, vm, om, grow, mlstm_norm_g[l])
    return _outffn(x, mod3, yat, ym, w_out[l], norm_ffn_g[l], w_gate[l], w_up[l], w_down[l], final_norm_g)
```
